```python
import jax, jax.numpy as jnp
from jax import lax
import numpy as np

D_MODEL = 4096
BATCH = 4
SEQ = 4096
DEPTH = 1

PLE_DIM = 256
HEAD_DIM = 128
MOBA_HEADS = D_MODEL // (2 * HEAD_DIM)
SB_HEADS = D_MODEL // (2 * HEAD_DIM)
MOBA_WIDTH = MOBA_HEADS * HEAD_DIM
SB_WIDTH = SB_HEADS * HEAD_DIM
MOBA_BLOCK = 256
MOBA_TOPK = 3
MOBA_Q_CHUNK = 16
SB_Q_BLOCK = 128
D_FF = ((8 * D_MODEL // 3 + 255) // 256) * 256
RMS_EPS = 1e-6
IN_SPLITS = [MOBA_WIDTH, MOBA_WIDTH, MOBA_WIDTH, SB_WIDTH, SB_WIDTH, SB_WIDTH, D_MODEL, D_MODEL]
IN_COLS = sum(IN_SPLITS)

kernel_name = "hybrid_moba_stickbreaking_gated_block"


def rms_norm(x, g):
    xf = x.astype(jnp.float32)
    xf = xf * lax.rsqrt(jnp.mean(xf * xf, axis=-1, keepdims=True) + RMS_EPS)
    return xf.astype(x.dtype) * g


def swiglu(x, w_gate, w_up, w_down):
    return (jax.nn.silu(x @ w_gate) * (x @ w_up)) @ w_down


def alibi_slopes(n_heads):
    return jnp.asarray(2.0 ** (-8.0 * np.arange(1, n_heads + 1) / n_heads), dtype=jnp.float32)


def split_heads(t, n_heads):
    b, s, _ = t.shape
    return t.reshape(b, s, n_heads, HEAD_DIM).transpose(0, 2, 1, 3)


def merge_heads(t):
    b, h, s, d = t.shape
    return t.transpose(0, 2, 1, 3).reshape(b, s, h * d)


def moba_attention(q, k, v):
    B, H, S, dh = q.shape
    L = MOBA_BLOCK
    nb = -(-S // L)
    pad = nb * L - S
    k_pad = jnp.pad(k, ((0, 0), (0, 0), (0, pad), (0, 0)))
    v_pad = jnp.pad(v, ((0, 0), (0, 0), (0, pad), (0, 0)))
    kb = k_pad.reshape(B, H, nb, L, dh)
    vb = v_pad.reshape(B, H, nb, L, dh)
    counts = np.minimum(L, S - np.arange(nb) * L).astype(np.float32)
    k_mean = kb.astype(jnp.float32).sum(axis=3) / jnp.asarray(counts)[:, None]
    gate = jnp.einsum('bhsd,bhnd->bhsn', q.astype(jnp.float32), k_mean)
    q_block = jnp.arange(S) // L
    fully_past = jnp.arange(nb)[None, :] < q_block[:, None]
    gate = jnp.where(fully_past, gate, -jnp.inf)
    n_sel = min(MOBA_TOPK, nb)
    _, sel = lax.top_k(gate, n_sel)
    sel_ok = sel < q_block[:, None]

    scale = dh ** -0.5
    slopes = alibi_slopes(H)
    C = MOBA_Q_CHUNK
    bi = jnp.arange(B)[:, None, None, None]
    hi = jnp.arange(H)[None, :, None, None]

    def chunk(c):
        t0 = c * C
        qc = lax.dynamic_slice_in_dim(q, t0, C, axis=2)
        sc = lax.dynamic_slice_in_dim(sel, t0, C, axis=2)
        okc = lax.dynamic_slice_in_dim(sel_ok, t0, C, axis=2)
        t = (t0 + jnp.arange(C)).astype(jnp.float32)
        k_sel = kb[bi, hi, sc]
        v_sel = vb[bi, hi, sc]
        s_sel = (sc[..., None] * L + jnp.arange(L)).astype(jnp.float32)
        logit_sel = (jnp.einsum('bhcd,bhcnld->bhcnl', qc, k_sel).astype(jnp.float32) * scale
                     - slopes[:, None, None, None] * jnp.abs(t[:, None, None] - s_sel))
        logit_sel = jnp.where(okc[..., None], logit_sel, -jnp.inf)
        b0 = t0 // L
        k_own = lax.dynamic_slice_in_dim(k_pad, b0 * L, L, axis=2)
        v_own = lax.dynamic_slice_in_dim(v_pad, b0 * L, L, axis=2)
        s_own = (b0 * L + jnp.arange(L)).astype(jnp.float32)
        logit_own = (jnp.einsum('bhcd,bhld->bhcl', qc, k_own).astype(jnp.float32) * scale
                     - slopes[:, None, None] * jnp.abs(t[:, None] - s_own[None, :]))
        logit_own = jnp.where(s_own[None, :] <= t[:, None], logit_own, -jnp.inf)
        logits = jnp.concatenate([logit_sel.reshape(B, H, C, n_sel * L), logit_own], axis=-1)
        probs = jax.nn.softmax(logits, axis=-1)
        p_sel = probs[..., :n_sel * L].reshape(B, H, C, n_sel, L).astype(v.dtype)
        p_own = probs[..., n_sel * L:].astype(v.dtype)
        return (jnp.einsum('bhcnl,bhcnld->bhcd', p_sel, v_sel)
                + jnp.einsum('bhcl,bhld->bhcd', p_own, v_own))

    outs = lax.map(chunk, jnp.arange(S // C))
    return outs.transpose(1, 2, 0, 3, 4).reshape(B, H, S, dh)


def stick_breaking_attention(q, k, v):
    B, H, S, dh = q.shape
    scale = dh ** -0.5
    outs = []
    for i in range(S // SB_Q_BLOCK):
        t0, t1 = i * SB_Q_BLOCK, (i + 1) * SB_Q_BLOCK
        z = jnp.einsum('bhqd,bhsd->bhqs', q[:, :, t0:t1], k[:, :, :t1]).astype(jnp.float32) * scale
        t = jnp.arange(t0, t1)
        s = jnp.arange(t1)
        strict = s[None, :] < t[:, None]
        log_keep = jnp.where(strict, jax.nn.log_sigmoid(-z), 0.0)
        later = lax.cumsum(log_keep, axis=3, reverse=True) - log_keep
        w = jnp.where(strict, jnp.exp(jax.nn.log_sigmoid(z) + later), 0.0)
        outs.append(jnp.einsum('bhqs,bhsd->bhqd', w.astype(v.dtype), v[:, :, :t1]))
    return jnp.concatenate(outs, axis=2)


def setup_inputs(seed: int = 0) -> dict:
    key = jax.random.key(seed)
    ks = jax.random.split(key, 24)
    f32 = jnp.float32

    def w(k, shape, fan_in):
        return jax.random.normal(k, shape, f32) * (fan_in ** -0.5)

    def gain(k, shape):
        return 1.0 + 0.02 * jax.random.normal(k, shape, f32)

    return {
        "x": jax.random.normal(ks[0], (BATCH, SEQ, D_MODEL), f32),
        "p": jax.random.normal(ks[1], (DEPTH, BATCH, SEQ, PLE_DIM), f32),
        "ffn1_norm": gain(ks[2], (DEPTH, D_MODEL)),
        "ffn1_w_gate": w(ks[3], (DEPTH, D_MODEL, D_FF), D_MODEL),
        "ffn1_w_up": w(ks[4], (DEPTH, D_MODEL, D_FF), D_MODEL),
        "ffn1_w_down": w(ks[5], (DEPTH, D_FF, D_MODEL), D_FF),
        "mix_norm": gain(ks[6], (DEPTH, D_MODEL)),
        "w_in": w(ks[7], (DEPTH, D_MODEL, IN_COLS), D_MODEL),
        "w_branch_moba": w(ks[8], (DEPTH, MOBA_WIDTH, D_MODEL), MOBA_WIDTH),
        "w_branch_sb": w(ks[9], (DEPTH, SB_WIDTH, D_MODEL), SB_WIDTH),
        "w_out": w(ks[10], (DEPTH, D_MODEL, D_MODEL), D_MODEL),
        "ffn2_norm": gain(ks[11], (DEPTH, D_MODEL)),
        "ffn2_w_gate": w(ks[12], (DEPTH, D_MODEL, D_FF), D_MODEL),
        "ffn2_w_up": w(ks[13], (DEPTH, D_MODEL, D_FF), D_MODEL),
        "ffn2_w_down": w(ks[14], (DEPTH, D_FF, D_MODEL), D_FF),
        "ple_norm": gain(ks[15], (DEPTH, D_MODEL)),
        "w_ple_gate": w(ks[16], (DEPTH, D_MODEL, D_MODEL), D_MODEL),
        "w_ple_proj": w(ks[17], (DEPTH, PLE_DIM, D_MODEL), PLE_DIM),
        "final_norm": gain(ks[18], (D_MODEL,)),
    }


def reference(x, p, ffn1_norm, ffn1_w_gate, ffn1_w_up, ffn1_w_down, mix_norm, w_in,
              w_branch_moba, w_branch_sb, w_out, ffn2_norm, ffn2_w_gate, ffn2_w_up,
              ffn2_w_down, ple_norm, w_ple_gate, w_ple_proj, final_norm):
    h = x
    offsets = [int(o) for o in np.cumsum(IN_SPLITS)[:-1]]
    for i in range(DEPTH):
        h = h + 0.5 * swiglu(rms_norm(h, ffn1_norm[i]), ffn1_w_gate[i], ffn1_w_up[i], ffn1_w_down[i])
        u = rms_norm(h, mix_norm[i])
        qa, ka, va, qb, kb, vb, ga, gb = jnp.split(u @ w_in[i], offsets, axis=-1)
        y_a = merge_heads(moba_attention(split_heads(qa, MOBA_HEADS), split_heads(ka, MOBA_HEADS),
                                         split_heads(va, MOBA_HEADS))) @ w_branch_moba[i]
        y_b = merge_heads(stick_breaking_attention(split_heads(qb, SB_HEADS), split_heads(kb, SB_HEADS),
                                                   split_heads(vb, SB_HEADS))) @ w_branch_sb[i]
        h = h + (jax.nn.sigmoid(ga) * y_a + jax.nn.sigmoid(gb) * y_b) @ w_out[i]
        h = h + 0.5 * swiglu(rms_norm(h, ffn2_norm[i]), ffn2_w_gate[i], ffn2_w_up[i], ffn2_w_down[i])
        h = h + jax.nn.sigmoid(rms_norm(h, ple_norm[i]) @ w_ple_gate[i]) * (p[i] @ w_ple_proj[i])
    return rms_norm(h, final_norm)
```

```python
import functools

import numpy as np
import jax
import jax.numpy as jnp
from jax import lax
from jax.experimental import pallas as pl
from jax.experimental.pallas import tpu as pltpu

F32 = jnp.float32
BF16 = jnp.bfloat16

HEAD_DIM = 128
MOBA_BLOCK = 256
MOBA_TOPK = 3
SB_TILE = 256
RMS_EPS = 1e-6
LANES = 128
NORM_ROWS = 64
VMEM_LIMIT_BYTES = 60 * 1024 * 1024


def _dot(a, b):
    return jnp.dot(a, b, preferred_element_type=F32)


def _dot_nt(a, b):
    return lax.dot_general(a, b, (((1,), (1,)), ((), ())), preferred_element_type=F32)


def _params(*semantics):
    return pltpu.CompilerParams(dimension_semantics=semantics, vmem_limit_bytes=VMEM_LIMIT_BYTES)


def _row_block(tm, d):
    return pl.BlockSpec((tm, d), lambda i, j: (i, 0), pipeline_mode=pl.Buffered(1))


def _rmsnorm_rows(x_ref, g_ref, out_ref):
    rows = x_ref.shape[0]
    chunk = min(NORM_ROWS, rows)

    def body(r, carry):
        sl = pl.ds(pl.multiple_of(r * chunk, chunk), chunk)
        x = x_ref[sl, :]
        ms = jnp.mean(x * x, axis=-1, keepdims=True)
        out_ref[sl, :] = ((x * lax.rsqrt(ms + RMS_EPS)) * g_ref[...]).astype(out_ref.dtype)
        return carry

    lax.fori_loop(0, rows // chunk, body, 0)


def _ffn_kernel(x_ref, g_ref, wg_ref, wu_ref, wd_ref, o_ref, xn_ref):
    j = pl.program_id(1)

    @pl.when(j == 0)
    def _():
        _rmsnorm_rows(x_ref, g_ref, xn_ref)
        o_ref[...] = jnp.zeros_like(o_ref)

    xn = xn_ref[...]
    gate = _dot(xn, wg_ref[...])
    up = _dot(xn, wu_ref[...])
    act = (gate * jax.nn.sigmoid(gate) * up).astype(BF16)
    o_ref[...] += _dot(act, wd_ref[...])

    @pl.when(j == pl.num_programs(1) - 1)
    def _():
        o_ref[...] = x_ref[...] + 0.5 * o_ref[...]


def _ffn(x, g, wg, wu, wd, *, tm, tf):
    n, d = x.shape
    dff = wg.shape[1]
    return pl.pallas_call(
        _ffn_kernel,
        grid=(n // tm, dff // tf),
        in_specs=[
            _row_block(tm, d),
            pl.BlockSpec((1, d), lambda i, j: (0, 0)),
            pl.BlockSpec((d, tf), lambda i, j: (0, j)),
            pl.BlockSpec((d, tf), lambda i, j: (0, j)),
            pl.BlockSpec((tf, d), lambda i, j: (j, 0)),
        ],
        out_specs=_row_block(tm, d),
        out_shape=jax.ShapeDtypeStruct((n, d), F32),
        scratch_shapes=[pltpu.VMEM((tm, d), BF16)],
        compiler_params=_params("parallel", "arbitrary"),
        name="ffn",
    )(x, g, wg, wu, wd)


def _norm_proj_kernel(x_ref, g_ref, w_ref, o_ref, xn_ref):
    @pl.when(pl.program_id(1) == 0)
    def _():
        _rmsnorm_rows(x_ref, g_ref, xn_ref)

    o_ref[...] = _dot(xn_ref[...], w_ref[...]).astype(o_ref.dtype)


def _norm_proj(x, g, w, *, tm, tn):
    n, d = x.shape
    c = w.shape[1]
    return pl.pallas_call(
        _norm_proj_kernel,
        grid=(n // tm, c // tn),
        in_specs=[
            _row_block(tm, d),
            pl.BlockSpec((1, d), lambda i, j: (0, 0)),
            pl.BlockSpec((d, tn), lambda i, j: (0, j)),
        ],
        out_specs=pl.BlockSpec((tm, tn), lambda i, j: (i, j)),
        out_shape=jax.ShapeDtypeStruct((n, c), BF16),
        scratch_shapes=[pltpu.VMEM((tm, d), BF16)],
        compiler_params=_params("parallel", "arbitrary"),
        name="norm_proj",
    )(x, g, w)


def _moba_kernel(slope_ref, q_ref, k_ref, v_ref, o_ref, kmean_ref, *, n_blocks):
    h = pl.program_id(1)
    qi = pl.program_id(2)
    blk = MOBA_BLOCK
    scale = HEAD_DIM ** -0.5
    neg_slope = -slope_ref[h]

    @pl.when(qi == 0)
    def _():
        kmean_ref[...] = jnp.zeros_like(kmean_ref)
        for n in range(n_blocks):
            kb = k_ref[n * blk:(n + 1) * blk, :].astype(F32)
            kmean_ref[n:n + 1, :] = jnp.sum(kb, axis=0, keepdims=True) * (1.0 / blk)

    q = q_ref[...]
    row = lax.broadcasted_iota(jnp.int32, (blk, blk), 0)
    col = lax.broadcasted_iota(jnp.int32, (blk, blk), 1)
    dist = (row - col).astype(F32)

    kmean = kmean_ref[...]
    km_hi = kmean.astype(BF16)
    km_lo = (kmean - km_hi.astype(F32)).astype(BF16)
    gate = _dot_nt(q, km_hi) + _dot_nt(q, km_lo)
    lane = lax.broadcasted_iota(jnp.int32, (blk, LANES), 1)
    neg_inf = jnp.float32(-jnp.inf)
    gate = jnp.where(lane < qi, gate, neg_inf)
    chosen = jnp.zeros((blk, LANES), jnp.bool_)
    for _ in range(MOBA_TOPK):
        top = jnp.max(gate, axis=1, keepdims=True)
        first = jnp.min(jnp.where(gate == top, lane, LANES), axis=1, keepdims=True)
        pick = (lane == first) & (lane < qi)
        chosen = chosen | pick
        gate = jnp.where(pick, neg_inf, gate)
    sel_bias = jnp.where(chosen, 0.0, neg_inf)

    q_start = pl.multiple_of(qi * blk, blk)
    k_own = k_ref[pl.ds(q_start, blk), :]
    v_own = v_ref[pl.ds(q_start, blk), :]
    logit = _dot_nt(q, k_own) * scale + neg_slope * dist
    logit = jnp.where(col <= row, logit, neg_inf)
    m0 = jnp.max(logit, axis=1, keepdims=True)
    p0 = jnp.exp(logit - m0)
    l0 = jnp.sum(p0, axis=1, keepdims=True)
    acc0 = _dot(p0.astype(BF16), v_own)

    def body(n, carry):
        m, l, acc = carry
        start = pl.multiple_of(n * blk, blk)
        k_n = k_ref[pl.ds(start, blk), :]
        v_n = v_ref[pl.ds(start, blk), :]
        picked = jnp.max(jnp.where(lane == n, sel_bias, neg_inf), axis=1, keepdims=True)
        offset = ((qi - n) * blk).astype(F32)
        s = _dot_nt(q, k_n) * scale + neg_slope * (dist + offset) + picked
        m_new = jnp.maximum(m, jnp.max(s, axis=1, keepdims=True))
        alpha = jnp.exp(m - m_new)
        p = jnp.exp(s - m_new)
        l = alpha * l + jnp.sum(p, axis=1, keepdims=True)
        acc = alpha * acc + _dot(p.astype(BF16), v_n)
        return m_new, l, acc

    _, l, acc = lax.fori_loop(0, qi, body, (m0, l0, acc0))
    o_ref[...] = (acc / l).astype(o_ref.dtype)


def _moba(proj, slopes, *, batch, seq, heads, q_col, k_col, v_col):
    n_blocks = seq // MOBA_BLOCK
    assert n_blocks <= LANES
    kernel = functools.partial(_moba_kernel, n_blocks=n_blocks)
    return pl.pallas_call(
        kernel,
        grid=(batch, heads, n_blocks),
        in_specs=[
            pl.BlockSpec(memory_space=pltpu.SMEM),
            pl.BlockSpec((None, MOBA_BLOCK, HEAD_DIM), lambda b, h, i: (b, i, q_col + h)),
            pl.BlockSpec((None, seq, HEAD_DIM), lambda b, h, i: (b, 0, k_col + h)),
            pl.BlockSpec((None, seq, HEAD_DIM), lambda b, h, i: (b, 0, v_col + h)),
        ],
        out_specs=pl.BlockSpec((None, MOBA_BLOCK, HEAD_DIM), lambda b, h, i: (b, i, h)),
        out_shape=jax.ShapeDtypeStruct((batch, seq, heads * HEAD_DIM), BF16),
        scratch_shapes=[pltpu.VMEM((LANES, HEAD_DIM), F32)],
        compiler_params=_params("parallel", "parallel", "arbitrary"),
        name="moba",
    )(slopes, proj, proj, proj)


def _sb_kernel(q_ref, k_ref, v_ref, o_ref):
    qi = pl.program_id(2)
    tile = SB_TILE
    scale = HEAD_DIM ** -0.5
    q = q_ref[...]
    row = lax.broadcasted_iota(jnp.int32, (tile, tile), 0)
    col = lax.broadcasted_iota(jnp.int32, (tile, tile), 1)
    suffix = (row > col).astype(BF16)

    def tile_terms(j, strict):
        start = pl.multiple_of(j * tile, tile)
        k_j = k_ref[pl.ds(start, tile), :]
        v_j = v_ref[pl.ds(start, tile), :]
        z = _dot_nt(q, k_j) * scale
        log_keep = -(jnp.maximum(z, 0.0) + jnp.log1p(jnp.exp(-jnp.abs(z))))
        if strict is not None:
            log_keep = jnp.where(strict, log_keep, 0.0)
        hi = log_keep.astype(BF16)
        lo = (log_keep - hi.astype(F32)).astype(BF16)
        later_in_tile = _dot(hi, suffix) + _dot(lo, suffix)
        row_sum = jnp.sum(log_keep, axis=1, keepdims=True)
        return z, log_keep, later_in_tile, row_sum, v_j

    strict = col < row
    z, log_keep, later, run, v_j = tile_terms(qi, strict)
    w = jnp.where(strict, jnp.exp(z + log_keep + later), 0.0)
    acc = _dot(w.astype(BF16), v_j)

    def body(step, carry):
        run, acc = carry
        z, log_keep, later, row_sum, v_j = tile_terms(qi - 1 - step, None)
        w = jnp.exp(z + log_keep + later + run)
        return run + row_sum, acc + _dot(w.astype(BF16), v_j)

    _, acc = lax.fori_loop(0, qi, body, (run, acc))
    o_ref[...] = acc.astype(o_ref.dtype)


def _stick_breaking(proj, *, batch, seq, heads, q_col, k_col, v_col):
    return pl.pallas_call(
        _sb_kernel,
        grid=(batch, heads, seq // SB_TILE),
        in_specs=[
            pl.BlockSpec((None, SB_TILE, HEAD_DIM), lambda b, h, i: (b, i, q_col + h)),
            pl.BlockSpec((None, seq, HEAD_DIM), lambda b, h, i: (b, 0, k_col + h)),
            pl.BlockSpec((None, seq, HEAD_DIM), lambda b, h, i: (b, 0, v_col + h)),
        ],
        out_specs=pl.BlockSpec((None, SB_TILE, HEAD_DIM), lambda b, h, i: (b, i, h)),
        out_shape=jax.ShapeDtypeStruct((batch, seq, heads * HEAD_DIM), BF16),
        compiler_params=_params("parallel", "parallel", "arbitrary"),
        name="stick_breaking",
    )(proj, proj, proj)


def _mix_kernel(a_ref, b_ref, wa_ref, wb_ref, ga_ref, gb_ref, o_ref):
    ya = _dot(a_ref[...], wa_ref[...])
    yb = _dot(b_ref[...], wb_ref[...])
    ga = jax.nn.sigmoid(ga_ref[...].astype(F32))
    gb = jax.nn.sigmoid(gb_ref[...].astype(F32))
    o_ref[...] = (ga * ya + gb * yb).astype(o_ref.dtype)


def _mix(att_a, att_b, wa, wb, proj, *, ga_col, gb_col, tm, tn):
    n, ka = att_a.shape
    kb = att_b.shape[1]
    d = wa.shape[1]
    return pl.pallas_call(
        _mix_kernel,
        grid=(n // tm, d // tn),
        in_specs=[
            pl.BlockSpec((tm, ka), lambda i, j: (i, 0)),
            pl.BlockSpec((tm, kb), lambda i, j: (i, 0)),
            pl.BlockSpec((ka, tn), lambda i, j: (0, j)),
            pl.BlockSpec((kb, tn), lambda i, j: (0, j)),
            pl.BlockSpec((tm, tn), lambda i, j: (i, ga_col // tn + j)),
            pl.BlockSpec((tm, tn), lambda i, j: (i, gb_col // tn + j)),
        ],
        out_specs=pl.BlockSpec((tm, tn), lambda i, j: (i, j)),
        out_shape=jax.ShapeDtypeStruct((n, d), BF16),
        compiler_params=_params("parallel", "arbitrary"),
        name="mix",
    )(att_a, att_b, wa, wb, proj, proj)


def _matmul_residual_kernel(a_ref, w_ref, r_ref, o_ref):
    o_ref[...] = r_ref[...] + _dot(a_ref[...], w_ref[...])


def _matmul_residual(a, w, r, *, tm, tn):
    n, k = a.shape
    d = w.shape[1]
    return pl.pallas_call(
        _matmul_residual_kernel,
        grid=(n // tm, d // tn),
        in_specs=[
            pl.BlockSpec((tm, k), lambda i, j: (i, 0)),
            pl.BlockSpec((k, tn), lambda i, j: (0, j)),
            pl.BlockSpec((tm, tn), lambda i, j: (i, j)),
        ],
        out_specs=pl.BlockSpec((tm, tn), lambda i, j: (i, j)),
        out_shape=jax.ShapeDtypeStruct((n, d), F32),
        compiler_params=_params("parallel", "arbitrary"),
        name="matmul_residual",
    )(a, w, r)


def _ple_kernel(h_ref, hc_ref, p_ref, g_ref, wg_ref, wp_ref, fg_ref, o_ref, xn_ref, new_ref, ssq_ref):
    j = pl.program_id(1)
    nj = pl.num_programs(1)
    tn = hc_ref.shape[1]

    @pl.when(j == 0)
    def _():
        _rmsnorm_rows(h_ref, g_ref, xn_ref)
        ssq_ref[...] = jnp.zeros_like(ssq_ref)

    gate = jax.nn.sigmoid(_dot(xn_ref[...], wg_ref[...]))
    emb = _dot(p_ref[...].astype(BF16), wp_ref[...])
    new = hc_ref[...] + gate * emb
    new_ref[j] = new
    ssq_ref[...] += jnp.sum(new * new, axis=-1, keepdims=True)

    @pl.when(j == nj - 1)
    def _():
        d = o_ref.shape[1]
        inv = lax.rsqrt(ssq_ref[...] * (1.0 / d) + RMS_EPS)
        for c in range(d // tn):
            o_ref[:, c * tn:(c + 1) * tn] = (new_ref[c] * inv) * fg_ref[:, c * tn:(c + 1) * tn]


def _ple(h, p, g, wg, wp, fg, *, tm, tn):
    n, d = h.shape
    pd = p.shape[1]
    return pl.pallas_call(
        _ple_kernel,
        grid=(n // tm, d // tn),
        in_specs=[
            _row_block(tm, d),
            pl.BlockSpec((tm, tn), lambda i, j: (i, j)),
            pl.BlockSpec((tm, pd), lambda i, j: (i, 0)),
            pl.BlockSpec((1, d), lambda i, j: (0, 0)),
            pl.BlockSpec((d, tn), lambda i, j: (0, j)),
            pl.BlockSpec((pd, tn), lambda i, j: (0, j)),
            pl.BlockSpec((1, d), lambda i, j: (0, 0)),
        ],
        out_specs=_row_block(tm, d),
        out_shape=jax.ShapeDtypeStruct((n, d), F32),
        scratch_shapes=[
            pltpu.VMEM((tm, d), BF16),
            pltpu.VMEM((d // tn, tm, tn), F32),
            pltpu.VMEM((tm, 1), F32),
        ],
        compiler_params=_params("parallel", "arbitrary"),
        name="ple_final_norm",
    )(h, h, p, g, wg, wp, fg)


def _tile(n, want):
    t = min(n, want)
    assert n % t == 0, (n, want)
    return t


def kernel(x, p, ffn1_norm, ffn1_w_gate, ffn1_w_up, ffn1_w_down, mix_norm, w_in, w_branch_moba, w_branch_sb, w_out, ffn2_norm, ffn2_w_gate, ffn2_w_up, ffn2_w_down, ple_norm, w_ple_gate, w_ple_proj, final_norm):
    batch, seq, d = x.shape
    depth = p.shape[0]
    n = batch * seq
    moba_w = w_branch_moba.shape[1]
    sb_w = w_branch_sb.shape[1]
    assert moba_w % HEAD_DIM == 0 and sb_w % HEAD_DIM == 0
    assert seq % MOBA_BLOCK == 0 and seq % SB_TILE == 0
    assert w_in.shape[2] == 3 * moba_w + 3 * sb_w + 2 * d
    moba_heads = moba_w // HEAD_DIM
    sb_heads = sb_w // HEAD_DIM
    slopes = jnp.asarray(2.0 ** (-8.0 * np.arange(1, moba_heads + 1) / moba_heads), dtype=F32)
    off = np.cumsum([0, moba_w, moba_w, moba_w, sb_w, sb_w, sb_w, d])
    col = [int(o) // HEAD_DIM for o in off]

    tm = _tile(n, 512)
    tn = _tile(d, 1024)
    tf = _tile(ffn1_w_gate.shape[2], 256)
    row = lambda v: v.reshape(1, -1)
    bf = lambda w: w.astype(BF16)

    assert depth == 1
    i = 0
    h = x.reshape(n, d)
    h = _ffn(h, row(ffn1_norm[i]), bf(ffn1_w_gate[i]), bf(ffn1_w_up[i]), bf(ffn1_w_down[i]), tm=tm, tf=tf)
    proj = _norm_proj(h, row(mix_norm[i]), bf(w_in[i]), tm=tm, tn=tn)
    proj3 = proj.reshape(batch, seq, proj.shape[1])
    att_a = _moba(proj3, slopes, batch=batch, seq=seq, heads=moba_heads,
                  q_col=col[0], k_col=col[1], v_col=col[2])
    att_b = _stick_breaking(proj3, batch=batch, seq=seq, heads=sb_heads,
                            q_col=col[3], k_col=col[4], v_col=col[5])
    mixed = _mix(att_a.reshape(n, moba_w), att_b.reshape(n, sb_w), bf(w_branch_moba[i]), bf(w_branch_sb[i]),
                 proj, ga_col=int(off[6]), gb_col=int(off[7]), tm=tm, tn=tn)
    h = _matmul_residual(mixed, bf(w_out[i]), h, tm=tm, tn=tn)
    h = _ffn(h, row(ffn2_norm[i]), bf(ffn2_w_gate[i]), bf(ffn2_w_up[i]), bf(ffn2_w_down[i]), tm=tm, tf=tf)
    h = _ple(h, p[i].reshape(n, -1), row(ple_norm[i]), bf(w_ple_gate[i]), bf(w_ple_proj[i]), row(final_norm),
             tm=tm, tn=tn)
    return h.reshape(batch, seq, d)
```

```python
import functools

import numpy as np
import jax
import jax.numpy as jnp
from jax import lax
from jax.experimental import pallas as pl
from jax.experimental.pallas import tpu as pltpu

F32 = jnp.float32
BF16 = jnp.bfloat16

HEAD_DIM = 128
MOBA_BLOCK = 256
MOBA_TOPK = 3
SB_TILE = 256
RMS_EPS = 1e-6
LANES = 128
NORM_ROWS = 64
VMEM_LIMIT_BYTES = 60 * 1024 * 1024
SUPER_TILES = 4
LOG2E = 1.4426950408889634
MASKED = -(2.0 ** 100)


def _dot(a, b):
    return jnp.dot(a, b, preferred_element_type=F32)


def _dot_nt(a, b):
    return lax.dot_general(a, b, (((1,), (1,)), ((), ())), preferred_element_type=F32)


def _params(*semantics):
    return pltpu.CompilerParams(dimension_semantics=semantics, vmem_limit_bytes=VMEM_LIMIT_BYTES)


def _row_block(tm, d):
    return pl.BlockSpec((tm, d), lambda i, j: (i, 0), pipeline_mode=pl.Buffered(1))


def _rmsnorm_rows(x_ref, g_ref, out_ref):
    rows = x_ref.shape[0]
    chunk = min(NORM_ROWS, rows)

    def body(r, carry):
        sl = pl.ds(pl.multiple_of(r * chunk, chunk), chunk)
        x = x_ref[sl, :]
        ms = jnp.mean(x * x, axis=-1, keepdims=True)
        out_ref[sl, :] = ((x * lax.rsqrt(ms + RMS_EPS)) * g_ref[...]).astype(out_ref.dtype)
        return carry

    lax.fori_loop(0, rows // chunk, body, 0)


def _ffn_kernel(x_ref, g_ref, wg_ref, wu_ref, wd_ref, o_ref, xn_ref):
    j = pl.program_id(1)

    @pl.when(j == 0)
    def _():
        _rmsnorm_rows(x_ref, g_ref, xn_ref)
        o_ref[...] = jnp.zeros_like(o_ref)

    xn = xn_ref[...]
    gate = _dot(xn, wg_ref[...])
    up = _dot(xn, wu_ref[...])
    act = (gate * jax.nn.sigmoid(gate) * up).astype(BF16)
    o_ref[...] += _dot(act, wd_ref[...])

    @pl.when(j == pl.num_programs(1) - 1)
    def _():
        o_ref[...] = x_ref[...] + 0.5 * o_ref[...]


def _ffn(x, g, wg, wu, wd, *, tm, tf):
    n, d = x.shape
    dff = wg.shape[1]
    return pl.pallas_call(
        _ffn_kernel,
        grid=(n // tm, dff // tf),
        in_specs=[
            _row_block(tm, d),
            pl.BlockSpec((1, d), lambda i, j: (0, 0)),
            pl.BlockSpec((d, tf), lambda i, j: (0, j)),
            pl.BlockSpec((d, tf), lambda i, j: (0, j)),
            pl.BlockSpec((tf, d), lambda i, j: (j, 0)),
        ],
        out_specs=_row_block(tm, d),
        out_shape=jax.ShapeDtypeStruct((n, d), F32),
        scratch_shapes=[pltpu.VMEM((tm, d), BF16)],
        compiler_params=_params("parallel", "arbitrary"),
        name="ffn",
    )(x, g, wg, wu, wd)


def _norm_proj_kernel(x_ref, g_ref, w_ref, o_ref, xn_ref):
    @pl.when(pl.program_id(1) == 0)
    def _():
        _rmsnorm_rows(x_ref, g_ref, xn_ref)

    o_ref[...] = _dot(xn_ref[...], w_ref[...]).astype(o_ref.dtype)


def _norm_proj(x, g, w, *, tm, tn):
    n, d = x.shape
    c = w.shape[1]
    return pl.pallas_call(
        _norm_proj_kernel,
        grid=(n // tm, c // tn),
        in_specs=[
            _row_block(tm, d),
            pl.BlockSpec((1, d), lambda i, j: (0, 0)),
            pl.BlockSpec((d, tn), lambda i, j: (0, j)),
        ],
        out_specs=pl.BlockSpec((tm, tn), lambda i, j: (i, j)),
        out_shape=jax.ShapeDtypeStruct((n, c), BF16),
        scratch_shapes=[pltpu.VMEM((tm, d), BF16)],
        compiler_params=_params("parallel", "arbitrary"),
        name="norm_proj",
    )(x, g, w)


def _moba_kernel(slope_ref, q_ref, k_ref, v_ref, o_ref, kaug_ref, vaug_ref, kmean_ref, qaug_ref, m_ref, acc_ref,
                 *, n_blocks, sub):
    h = pl.program_id(1)
    qi = pl.program_id(2)
    blk = MOBA_BLOCK
    dh = HEAD_DIM
    tq = sub * blk
    inv_scale = dh ** 0.5
    c2 = LOG2E / inv_scale
    bias_lane = LANES - 3

    @pl.when(qi == 0)
    def _():
        lane = lax.broadcasted_iota(jnp.int32, (blk, LANES), 1)
        pos = lax.broadcasted_iota(jnp.int32, (blk, LANES), 0)
        unit = slope_ref[h] * inv_scale
        kmean_ref[...] = jnp.zeros_like(kmean_ref)
        for n in range(n_blocks):
            rows = slice(n * blk, (n + 1) * blk)
            beta = (pos + n * blk).astype(F32) * unit
            b_hi = beta.astype(BF16).astype(F32)
            b_mid = (beta - b_hi).astype(BF16).astype(F32)
            b_lo = beta - b_hi - b_mid
            base = jnp.where(lane == bias_lane, b_hi,
                             jnp.where(lane == bias_lane + 1, b_mid,
                                       jnp.where(lane == bias_lane + 2, b_lo, 0.0)))
            kb = k_ref[rows, :]
            kaug_ref[rows, 0:dh] = kb
            kaug_ref[rows, dh:2 * dh] = jnp.where(lane == n, 1.0, base).astype(BF16)
            kmean_ref[n:n + 1, :] = jnp.sum(kb.astype(F32), axis=0, keepdims=True) * (1.0 / blk)
            vaug_ref[rows, 0:dh] = v_ref[rows, :]
            vaug_ref[rows, dh:2 * dh] = jnp.ones((blk, dh), BF16)

    q = q_ref[...]
    kmean = kmean_ref[...]
    km_hi = kmean.astype(BF16)
    km_lo = (kmean - km_hi.astype(F32)).astype(BF16)
    gate = _dot_nt(q, km_hi) + _dot_nt(q, km_lo)
    lane = lax.broadcasted_iota(jnp.int32, (tq, LANES), 1)
    own = qi * sub + lax.broadcasted_iota(jnp.int32, (tq, LANES), 0) // blk
    neg_inf = jnp.float32(-jnp.inf)
    gate = jnp.where(lane < own, gate, neg_inf)
    chosen = lane == own
    for _ in range(MOBA_TOPK):
        top = jnp.max(gate, axis=1, keepdims=True)
        first = jnp.min(jnp.where(gate == top, lane, LANES), axis=1, keepdims=True)
        pick = (lane == first) & (lane < own)
        chosen = chosen | pick
        gate = jnp.where(pick, neg_inf, gate)
    extra = jnp.where(lane >= bias_lane, 1.0, jnp.where(chosen, 0.0, MASKED))
    qaug_ref[:, 0:dh] = q
    qaug_ref[:, dh:2 * dh] = extra.astype(BF16)
    m_ref[...] = jnp.full_like(m_ref, MASKED)
    acc_ref[...] = jnp.zeros_like(acc_ref)

    def scores(row_lo, n, causal):
        start = pl.multiple_of(n * blk, blk)
        s = _dot_nt(qaug_ref[row_lo:tq, :], kaug_ref[pl.ds(start, blk), :])
        if causal:
            r = lax.broadcasted_iota(jnp.int32, s.shape, 0)
            c = lax.broadcasted_iota(jnp.int32, s.shape, 1)
            s = jnp.where(c <= r, s, MASKED)
        return s

    def max_pass(row_lo, n, causal):
        m_ref[row_lo:tq, :] = jnp.maximum(m_ref[row_lo:tq, :], scores(row_lo, n, causal))

    def sum_pass(row_lo, n, causal):
        start = pl.multiple_of(n * blk, blk)
        p = jnp.exp2(scores(row_lo, n, causal) * c2 - m_ref[row_lo:tq, :])
        acc_ref[row_lo:tq, :] += _dot(p.astype(BF16), vaug_ref[pl.ds(start, blk), :])

    def all_blocks(visit):
        for kk in range(sub):
            visit(kk * blk, qi * sub + kk, True)

        def body(i, carry):
            for u in range(sub):
                visit(0, i * sub + u, False)
            return carry

        lax.fori_loop(0, qi, body, 0)

    all_blocks(max_pass)
    m_ref[...] = jnp.broadcast_to(jnp.max(m_ref[...], axis=1, keepdims=True) * c2, m_ref.shape)
    all_blocks(sum_pass)
    acc = acc_ref[...]
    o_ref[...] = (acc[:, 0:dh] / acc[:, dh:2 * dh]).astype(o_ref.dtype)


def _moba(proj, slopes, *, batch, seq, heads, sub, q_col, k_col, v_col):
    n_blocks = seq // MOBA_BLOCK
    assert n_blocks <= LANES - 3 and n_blocks % sub == 0
    tq = sub * MOBA_BLOCK
    kernel = functools.partial(_moba_kernel, n_blocks=n_blocks, sub=sub)
    return pl.pallas_call(
        kernel,
        grid=(batch, heads, n_blocks // sub),
        in_specs=[
            pl.BlockSpec(memory_space=pltpu.SMEM),
            pl.BlockSpec((None, tq, HEAD_DIM), lambda b, h, i: (b, i, q_col + h)),
            pl.BlockSpec((None, seq, HEAD_DIM), lambda b, h, i: (b, 0, k_col + h)),
            pl.BlockSpec((None, seq, HEAD_DIM), lambda b, h, i: (b, 0, v_col + h)),
        ],
        out_specs=pl.BlockSpec((None, tq, HEAD_DIM), lambda b, h, i: (b, i, h)),
        out_shape=jax.ShapeDtypeStruct((batch, seq, heads * HEAD_DIM), BF16),
        scratch_shapes=[
            pltpu.VMEM((seq, 2 * HEAD_DIM), BF16),
            pltpu.VMEM((seq, 2 * HEAD_DIM), BF16),
            pltpu.VMEM((LANES, HEAD_DIM), F32),
            pltpu.VMEM((tq, 2 * HEAD_DIM), BF16),
            pltpu.VMEM((tq, MOBA_BLOCK), F32),
            pltpu.VMEM((tq, 2 * HEAD_DIM), F32),
        ],
        compiler_params=_params("parallel", "parallel", "arbitrary"),
        name="moba",
    )(slopes, proj, proj, proj)


def _sb_kernel(q_ref, k_ref, v_ref, o_ref, run_ref, acc_ref, *, sub):
    qi = pl.program_id(2)
    tile = SB_TILE
    tq = sub * tile
    c2 = LOG2E * HEAD_DIM ** -0.5
    row2 = lax.broadcasted_iota(jnp.int32, (2 * tile, tile), 0)
    col2 = lax.broadcasted_iota(jnp.int32, (2 * tile, tile), 1)
    suffix2 = ((row2 % tile) > col2).astype(BF16)
    sign_bit = jnp.uint32(0x80000000)
    run_ref[...] = jnp.zeros_like(run_ref)
    acc_ref[...] = jnp.zeros_like(acc_ref)

    def sweep(row_lo, j, diagonal):
        rows = slice(row_lo, tq)
        start = pl.multiple_of(j * tile, tile)
        z2 = _dot_nt(q_ref[rows, :], k_ref[pl.ds(start, tile), :]) * c2
        neg_abs = lax.bitcast_convert_type(lax.bitcast_convert_type(z2, jnp.uint32) | sign_bit, F32)
        cost = jnp.maximum(z2, 0.0) + jnp.log(1.0 + jnp.exp2(neg_abs)) * LOG2E
        if diagonal:
            r = lax.broadcasted_iota(jnp.int32, z2.shape, 0)
            c = lax.broadcasted_iota(jnp.int32, z2.shape, 1)
            strict = c < r
            cost = jnp.where(strict, cost, 0.0)
        hi = cost.astype(BF16)
        lo = (cost - hi.astype(F32)).astype(BF16)
        later = _dot(jnp.concatenate([hi, lo], axis=1), suffix2)
        w = jnp.exp2(z2 - cost - later - run_ref[rows, :])
        if diagonal:
            w = jnp.where(strict, w, 0.0)
        acc_ref[rows, :] += _dot(w.astype(BF16), v_ref[pl.ds(start, tile), :])
        run_ref[rows, :] += jnp.sum(cost, axis=1, keepdims=True)

    for kk in reversed(range(sub)):
        sweep(kk * tile, qi * sub + kk, True)

    unroll = 2 if sub % 2 == 0 else 1

    def body(step, carry):
        for u in range(unroll):
            sweep(0, qi * sub - 1 - (step * unroll + u), False)
        return carry

    lax.fori_loop(0, qi * (sub // unroll), body, 0)
    o_ref[...] = acc_ref[...].astype(o_ref.dtype)


def _stick_breaking(proj, *, batch, seq, heads, sub, q_col, k_col, v_col):
    n_tiles = seq // SB_TILE
    assert n_tiles % sub == 0
    tq = sub * SB_TILE
    return pl.pallas_call(
        functools.partial(_sb_kernel, sub=sub),
        grid=(batch, heads, n_tiles // sub),
        in_specs=[
            pl.BlockSpec((None, tq, HEAD_DIM), lambda b, h, i: (b, i, q_col + h)),
            pl.BlockSpec((None, seq, HEAD_DIM), lambda b, h, i: (b, 0, k_col + h)),
            pl.BlockSpec((None, seq, HEAD_DIM), lambda b, h, i: (b, 0, v_col + h)),
        ],
        out_specs=pl.BlockSpec((None, tq, HEAD_DIM), lambda b, h, i: (b, i, h)),
        out_shape=jax.ShapeDtypeStruct((batch, seq, heads * HEAD_DIM), BF16),
        scratch_shapes=[pltpu.VMEM((tq, 1), F32), pltpu.VMEM((tq, HEAD_DIM), F32)],
        compiler_params=_params("parallel", "parallel", "arbitrary"),
        name="stick_breaking",
    )(proj, proj, proj)


def _mix_kernel(a_ref, b_ref, wa_ref, wb_ref, ga_ref, gb_ref, o_ref):
    ya = _dot(a_ref[...], wa_ref[...])
    yb = _dot(b_ref[...], wb_ref[...])
    ga = jax.nn.sigmoid(ga_ref[...].astype(F32))
    gb = jax.nn.sigmoid(gb_ref[...].astype(F32))
    o_ref[...] = (ga * ya + gb * yb).astype(o_ref.dtype)


def _mix(att_a, att_b, wa, wb, proj, *, ga_col, gb_col, tm, tn):
    n, ka = att_a.shape
    kb = att_b.shape[1]
    d = wa.shape[1]
    return pl.pallas_call(
        _mix_kernel,
        grid=(n // tm, d // tn),
        in_specs=[
            pl.BlockSpec((tm, ka), lambda i, j: (i, 0)),
            pl.BlockSpec((tm, kb), lambda i, j: (i, 0)),
            pl.BlockSpec((ka, tn), lambda i, j: (0, j)),
            pl.BlockSpec((kb, tn), lambda i, j: (0, j)),
            pl.BlockSpec((tm, tn), lambda i, j: (i, ga_col // tn + j)),
            pl.BlockSpec((tm, tn), lambda i, j: (i, gb_col // tn + j)),
        ],
        out_specs=pl.BlockSpec((tm, tn), lambda i, j: (i, j)),
        out_shape=jax.ShapeDtypeStruct((n, d), BF16),
        compiler_params=_params("parallel", "arbitrary"),
        name="mix",
    )(att_a, att_b, wa, wb, proj, proj)


def _matmul_residual_kernel(a_ref, w_ref, r_ref, o_ref):
    o_ref[...] = r_ref[...] + _dot(a_ref[...], w_ref[...])


def _matmul_residual(a, w, r, *, tm, tn):
    n, k = a.shape
    d = w.shape[1]
    return pl.pallas_call(
        _matmul_residual_kernel,
        grid=(n // tm, d // tn),
        in_specs=[
            pl.BlockSpec((tm, k), lambda i, j: (i, 0)),
            pl.BlockSpec((k, tn), lambda i, j: (0, j)),
            pl.BlockSpec((tm, tn), lambda i, j: (i, j)),
        ],
        out_specs=pl.BlockSpec((tm, tn), lambda i, j: (i, j)),
        out_shape=jax.ShapeDtypeStruct((n, d), F32),
        compiler_params=_params("parallel", "arbitrary"),
        name="matmul_residual",
    )(a, w, r)


def _ple_kernel(h_ref, hc_ref, p_ref, g_ref, wg_ref, wp_ref, fg_ref, o_ref, xn_ref, new_ref, ssq_ref):
    j = pl.program_id(1)
    nj = pl.num_programs(1)
    tn = hc_ref.shape[1]

    @pl.when(j == 0)
    def _():
        _rmsnorm_rows(h_ref, g_ref, xn_ref)
        ssq_ref[...] = jnp.zeros_like(ssq_ref)

    gate = jax.nn.sigmoid(_dot(xn_ref[...], wg_ref[...]))
    emb = _dot(p_ref[...].astype(BF16), wp_ref[...])
    new = hc_ref[...] + gate * emb
    new_ref[j] = new
    ssq_ref[...] += jnp.sum(new * new, axis=-1, keepdims=True)

    @pl.when(j == nj - 1)
    def _():
        d = o_ref.shape[1]
        inv = lax.rsqrt(ssq_ref[...] * (1.0 / d) + RMS_EPS)
        for c in range(d // tn):
            o_ref[:, c * tn:(c + 1) * tn] = (new_ref[c] * inv) * fg_ref[:, c * tn:(c + 1) * tn]


def _ple(h, p, g, wg, wp, fg, *, tm, tn):
    n, d = h.shape
    pd = p.shape[1]
    return pl.pallas_call(
        _ple_kernel,
        grid=(n // tm, d // tn),
        in_specs=[
            _row_block(tm, d),
            pl.BlockSpec((tm, tn), lambda i, j: (i, j)),
            pl.BlockSpec((tm, pd), lambda i, j: (i, 0)),
            pl.BlockSpec((1, d), lambda i, j: (0, 0)),
            pl.BlockSpec((d, tn), lambda i, j: (0, j)),
            pl.BlockSpec((pd, tn), lambda i, j: (0, j)),
            pl.BlockSpec((1, d), lambda i, j: (0, 0)),
        ],
        out_specs=_row_block(tm, d),
        out_shape=jax.ShapeDtypeStruct((n, d), F32),
        scratch_shapes=[
            pltpu.VMEM((tm, d), BF16),
            pltpu.VMEM((d // tn, tm, tn), F32),
            pltpu.VMEM((tm, 1), F32),
        ],
        compiler_params=_params("parallel", "arbitrary"),
        name="ple_final_norm",
    )(h, h, p, g, wg, wp, fg)


def _tile(n, want):
    t = min(n, want)
    assert n % t == 0, (n, want)
    return t


def kernel(x, p, ffn1_norm, ffn1_w_gate, ffn1_w_up, ffn1_w_down, mix_norm, w_in, w_branch_moba, w_branch_sb, w_out, ffn2_norm, ffn2_w_gate, ffn2_w_up, ffn2_w_down, ple_norm, w_ple_gate, w_ple_proj, final_norm):
    batch, seq, d = x.shape
    depth = p.shape[0]
    n = batch * seq
    moba_w = w_branch_moba.shape[1]
    sb_w = w_branch_sb.shape[1]
    assert moba_w % HEAD_DIM == 0 and sb_w % HEAD_DIM == 0
    assert seq % MOBA_BLOCK == 0 and seq % SB_TILE == 0
    assert w_in.shape[2] == 3 * moba_w + 3 * sb_w + 2 * d
    moba_heads = moba_w // HEAD_DIM
    sb_heads = sb_w // HEAD_DIM
    slopes = jnp.asarray(2.0 ** (-8.0 * np.arange(1, moba_heads + 1) / moba_heads), dtype=F32)
    off = np.cumsum([0, moba_w, moba_w, moba_w, sb_w, sb_w, sb_w, d])
    col = [int(o) // HEAD_DIM for o in off]

    tm = _tile(n, 512)
    tm_wide = _tile(n, 1024)
    tn = _tile(d, 1024)
    tf = _tile(ffn1_w_gate.shape[2], 256)
    row = lambda v: v.reshape(1, -1)
    bf = lambda w: w.astype(BF16)

    sub = min(SUPER_TILES, seq // MOBA_BLOCK)
    assert depth == 1
    i = 0
    h = x.reshape(n, d)
    h = _ffn(h, row(ffn1_norm[i]), bf(ffn1_w_gate[i]), bf(ffn1_w_up[i]), bf(ffn1_w_down[i]), tm=tm_wide, tf=tf)
    proj = _norm_proj(h, row(mix_norm[i]), bf(w_in[i]), tm=tm_wide, tn=tn)
    proj3 = proj.reshape(batch, seq, proj.shape[1])
    att_a = _moba(proj3, slopes, batch=batch, seq=seq, heads=moba_heads, sub=sub,
                  q_col=col[0], k_col=col[1], v_col=col[2])
    att_b = _stick_breaking(proj3, batch=batch, seq=seq, heads=sb_heads, sub=sub,
                            q_col=col[3], k_col=col[4], v_col=col[5])
    mixed = _mix(att_a.reshape(n, moba_w), att_b.reshape(n, sb_w), bf(w_branch_moba[i]), bf(w_branch_sb[i]),
                 proj, ga_col=int(off[6]), gb_col=int(off[7]), tm=tm, tn=tn)
    h = _matmul_residual(mixed, bf(w_out[i]), h, tm=tm, tn=tn)
    h = _ffn(h, row(ffn2_norm[i]), bf(ffn2_w_gate[i]), bf(ffn2_w_up[i]), bf(ffn2_w_down[i]), tm=tm_wide, tf=tf)
    h = _ple(h, p[i].reshape(n, -1), row(ple_norm[i]), bf(w_ple_gate[i]), bf(w_ple_proj[i]), row(final_norm),
             tm=tm, tn=tn)
    return h.reshape(batch, seq, d)
```

```python
import functools

import numpy as np
import jax
import jax.numpy as jnp
from jax import lax
from jax.experimental import pallas as pl
from jax.experimental.pallas import tpu as pltpu

F32 = jnp.float32
BF16 = jnp.bfloat16

HEAD_DIM = 128
MOBA_BLOCK = 256
MOBA_TOPK = 3
SB_TILE = 256
RMS_EPS = 1e-6
LANES = 128
NORM_ROWS = 64
VMEM_LIMIT_BYTES = 60 * 1024 * 1024
SUPER_TILES = 4
SB_SUPER_TILES = 2
LOG2E = 1.4426950408889634
MASKED = -(2.0 ** 100)
SB_DEAD_BITS = 150.0


def _dot(a, b):
    return jnp.dot(a, b, preferred_element_type=F32)


def _dot_nt(a, b):
    return lax.dot_general(a, b, (((1,), (1,)), ((), ())), preferred_element_type=F32)


def _params(*semantics):
    return pltpu.CompilerParams(dimension_semantics=semantics, vmem_limit_bytes=VMEM_LIMIT_BYTES)


def _row_block(tm, d):
    return pl.BlockSpec((tm, d), lambda i, j: (i, 0), pipeline_mode=pl.Buffered(1))


def _rmsnorm_rows(x_ref, g_ref, out_ref):
    rows = x_ref.shape[0]
    chunk = min(NORM_ROWS, rows)

    def body(r, carry):
        sl = pl.ds(pl.multiple_of(r * chunk, chunk), chunk)
        x = x_ref[sl, :]
        ms = jnp.mean(x * x, axis=-1, keepdims=True)
        out_ref[sl, :] = ((x * lax.rsqrt(ms + RMS_EPS)) * g_ref[...]).astype(out_ref.dtype)
        return carry

    lax.fori_loop(0, rows // chunk, body, 0)


def _ffn_kernel(x_ref, g_ref, wg_ref, wu_ref, wd_ref, o_ref, xn_ref):
    j = pl.program_id(1)

    @pl.when(j == 0)
    def _():
        _rmsnorm_rows(x_ref, g_ref, xn_ref)
        o_ref[...] = jnp.zeros_like(o_ref)

    xn = xn_ref[...]
    gate = _dot(xn, wg_ref[...])
    up = _dot(xn, wu_ref[...])
    act = (gate * jax.nn.sigmoid(gate) * up).astype(BF16)
    o_ref[...] += _dot(act, wd_ref[...])

    @pl.when(j == pl.num_programs(1) - 1)
    def _():
        o_ref[...] = x_ref[...] + 0.5 * o_ref[...]


def _ffn(x, g, wg, wu, wd, *, tm, tf):
    n, d = x.shape
    dff = wg.shape[1]
    return pl.pallas_call(
        _ffn_kernel,
        grid=(n // tm, dff // tf),
        in_specs=[
            _row_block(tm, d),
            pl.BlockSpec((1, d), lambda i, j: (0, 0)),
            pl.BlockSpec((d, tf), lambda i, j: (0, j)),
            pl.BlockSpec((d, tf), lambda i, j: (0, j)),
            pl.BlockSpec((tf, d), lambda i, j: (j, 0)),
        ],
        out_specs=_row_block(tm, d),
        out_shape=jax.ShapeDtypeStruct((n, d), F32),
        scratch_shapes=[pltpu.VMEM((tm, d), BF16)],
        compiler_params=_params("parallel", "arbitrary"),
        name="ffn",
    )(x, g, wg, wu, wd)


def _norm_proj_kernel(x_ref, g_ref, w_ref, o_ref, xn_ref):
    @pl.when(pl.program_id(1) == 0)
    def _():
        _rmsnorm_rows(x_ref, g_ref, xn_ref)

    o_ref[...] = _dot(xn_ref[...], w_ref[...]).astype(o_ref.dtype)


def _norm_proj(x, g, w, *, tm, tn):
    n, d = x.shape
    c = w.shape[1]
    return pl.pallas_call(
        _norm_proj_kernel,
        grid=(n // tm, c // tn),
        in_specs=[
            _row_block(tm, d),
            pl.BlockSpec((1, d), lambda i, j: (0, 0)),
            pl.BlockSpec((d, tn), lambda i, j: (0, j)),
        ],
        out_specs=pl.BlockSpec((tm, tn), lambda i, j: (i, j)),
        out_shape=jax.ShapeDtypeStruct((n, c), BF16),
        scratch_shapes=[pltpu.VMEM((tm, d), BF16)],
        compiler_params=_params("parallel", "arbitrary"),
        name="norm_proj",
    )(x, g, w)


def _moba_kernel(slope_ref, q_ref, k_ref, v_ref, o_ref, kaug_ref, vaug_ref, kmean_ref, qaug_ref, m_ref, acc_ref,
                 *, n_blocks, sub):
    h = pl.program_id(1)
    qi = pl.program_id(2)
    blk = MOBA_BLOCK
    dh = HEAD_DIM
    tq = sub * blk
    inv_scale = dh ** 0.5
    c2 = LOG2E / inv_scale
    bias_lane = LANES - 3

    @pl.when(qi == 0)
    def _():
        lane = lax.broadcasted_iota(jnp.int32, (blk, LANES), 1)
        pos = lax.broadcasted_iota(jnp.int32, (blk, LANES), 0)
        unit = slope_ref[h] * inv_scale
        kmean_ref[...] = jnp.zeros_like(kmean_ref)
        for n in range(n_blocks):
            rows = slice(n * blk, (n + 1) * blk)
            beta = (pos + n * blk).astype(F32) * unit
            b_hi = beta.astype(BF16).astype(F32)
            b_mid = (beta - b_hi).astype(BF16).astype(F32)
            b_lo = beta - b_hi - b_mid
            base = jnp.where(lane == bias_lane, b_hi,
                             jnp.where(lane == bias_lane + 1, b_mid,
                                       jnp.where(lane == bias_lane + 2, b_lo, 0.0)))
            kb = k_ref[rows, :]
            kaug_ref[rows, 0:dh] = kb
            kaug_ref[rows, dh:2 * dh] = jnp.where(lane == n, 1.0, base).astype(BF16)
            kmean_ref[n:n + 1, :] = jnp.sum(kb.astype(F32), axis=0, keepdims=True) * (1.0 / blk)
            vaug_ref[rows, 0:dh] = v_ref[rows, :]
            vaug_ref[rows, dh:2 * dh] = jnp.ones((blk, dh), BF16)

    q = q_ref[...]
    kmean = kmean_ref[...]
    km_hi = kmean.astype(BF16)
    km_lo = (kmean - km_hi.astype(F32)).astype(BF16)
    gate = _dot_nt(q, km_hi) + _dot_nt(q, km_lo)
    lane = lax.broadcasted_iota(jnp.int32, (tq, LANES), 1)
    own = qi * sub + lax.broadcasted_iota(jnp.int32, (tq, LANES), 0) // blk
    neg_inf = jnp.float32(-jnp.inf)
    gate = jnp.where(lane < own, gate, neg_inf)
    chosen = lane == own
    for _ in range(MOBA_TOPK):
        top = jnp.max(gate, axis=1, keepdims=True)
        first = jnp.min(jnp.where(gate == top, lane, LANES), axis=1, keepdims=True)
        pick = (lane == first) & (lane < own)
        chosen = chosen | pick
        gate = jnp.where(pick, neg_inf, gate)
    extra = jnp.where(lane >= bias_lane, 1.0, jnp.where(chosen, 0.0, MASKED))
    qaug_ref[:, 0:dh] = q
    qaug_ref[:, dh:2 * dh] = extra.astype(BF16)
    m_ref[...] = jnp.full_like(m_ref, MASKED)
    acc_ref[...] = jnp.zeros_like(acc_ref)

    def scores(row_lo, n, causal):
        start = pl.multiple_of(n * blk, blk)
        s = _dot_nt(qaug_ref[row_lo:tq, :], kaug_ref[pl.ds(start, blk), :])
        if causal:
            r = lax.broadcasted_iota(jnp.int32, s.shape, 0)
            c = lax.broadcasted_iota(jnp.int32, s.shape, 1)
            s = jnp.where(c <= r, s, MASKED)
        return s

    def max_pass(row_lo, n, causal):
        m_ref[row_lo:tq, :] = jnp.maximum(m_ref[row_lo:tq, :], scores(row_lo, n, causal))

    def sum_pass(row_lo, n, causal):
        start = pl.multiple_of(n * blk, blk)
        p = jnp.exp2(scores(row_lo, n, causal) * c2 - m_ref[row_lo:tq, :])
        acc_ref[row_lo:tq, :] += _dot(p.astype(BF16), vaug_ref[pl.ds(start, blk), :])

    def all_blocks(visit):
        for kk in range(sub):
            visit(kk * blk, qi * sub + kk, True)

        def body(i, carry):
            for u in range(sub):
                visit(0, i * sub + u, False)
            return carry

        lax.fori_loop(0, qi, body, 0)

    all_blocks(max_pass)
    m_ref[...] = jnp.broadcast_to(jnp.max(m_ref[...], axis=1, keepdims=True) * c2, m_ref.shape)
    all_blocks(sum_pass)
    acc = acc_ref[...]
    o_ref[...] = (acc[:, 0:dh] / acc[:, dh:2 * dh]).astype(o_ref.dtype)


def _moba(proj, slopes, *, batch, seq, heads, sub, q_col, k_col, v_col):
    n_blocks = seq // MOBA_BLOCK
    assert n_blocks <= LANES - 3 and n_blocks % sub == 0
    tq = sub * MOBA_BLOCK
    kernel = functools.partial(_moba_kernel, n_blocks=n_blocks, sub=sub)
    return pl.pallas_call(
        kernel,
        grid=(batch, heads, n_blocks // sub),
        in_specs=[
            pl.BlockSpec(memory_space=pltpu.SMEM),
            pl.BlockSpec((None, tq, HEAD_DIM), lambda b, h, i: (b, i, q_col + h)),
            pl.BlockSpec((None, seq, HEAD_DIM), lambda b, h, i: (b, 0, k_col + h)),
            pl.BlockSpec((None, seq, HEAD_DIM), lambda b, h, i: (b, 0, v_col + h)),
        ],
        out_specs=pl.BlockSpec((None, tq, HEAD_DIM), lambda b, h, i: (b, i, h)),
        out_shape=jax.ShapeDtypeStruct((batch, seq, heads * HEAD_DIM), BF16),
        scratch_shapes=[
            pltpu.VMEM((seq, 2 * HEAD_DIM), BF16),
            pltpu.VMEM((seq, 2 * HEAD_DIM), BF16),
            pltpu.VMEM((LANES, HEAD_DIM), F32),
            pltpu.VMEM((tq, 2 * HEAD_DIM), BF16),
            pltpu.VMEM((tq, MOBA_BLOCK), F32),
            pltpu.VMEM((tq, 2 * HEAD_DIM), F32),
        ],
        compiler_params=_params("parallel", "parallel", "arbitrary"),
        name="moba",
    )(slopes, proj, proj, proj)


def _sb_kernel(q_ref, k_ref, v_ref, o_ref, run_ref, acc_ref, *, sub):
    qi = pl.program_id(2)
    tile = SB_TILE
    tq = sub * tile
    c2 = LOG2E * HEAD_DIM ** -0.5
    row2 = lax.broadcasted_iota(jnp.int32, (2 * tile, tile), 0)
    col2 = lax.broadcasted_iota(jnp.int32, (2 * tile, tile), 1)
    suffix2 = ((row2 % tile) > col2).astype(BF16)
    sign_bit = jnp.uint32(0x80000000)
    run_ref[...] = jnp.zeros_like(run_ref)
    acc_ref[...] = jnp.zeros_like(acc_ref)

    def sweep(row_lo, j, diagonal):
        rows = slice(row_lo, tq)
        start = pl.multiple_of(j * tile, tile)
        z2 = _dot_nt(q_ref[rows, :], k_ref[pl.ds(start, tile), :]) * c2
        neg_abs = lax.bitcast_convert_type(lax.bitcast_convert_type(z2, jnp.uint32) | sign_bit, F32)
        cost = jnp.maximum(z2, 0.0) + jnp.log(1.0 + jnp.exp2(neg_abs)) * LOG2E
        if diagonal:
            r = lax.broadcasted_iota(jnp.int32, z2.shape, 0)
            c = lax.broadcasted_iota(jnp.int32, z2.shape, 1)
            strict = c < r
            cost = jnp.where(strict, cost, 0.0)
        hi = cost.astype(BF16)
        lo = (cost - hi.astype(F32)).astype(BF16)
        later = _dot(jnp.concatenate([hi, lo], axis=1), suffix2)
        w = jnp.exp2(z2 - cost - later - run_ref[rows, :])
        if diagonal:
            w = jnp.where(strict, w, 0.0)
        acc_ref[rows, :] += _dot(w.astype(BF16), v_ref[pl.ds(start, tile), :])
        run_ref[rows, :] += jnp.sum(cost, axis=1, keepdims=True)

    for kk in reversed(range(sub)):
        sweep(kk * tile, qi * sub + kk, True)

    def alive():
        return jnp.min(run_ref[...]) <= SB_DEAD_BITS

    def cond(carry):
        step, go = carry
        return jnp.logical_and(step < qi * sub, go)

    def body(carry):
        step, _ = carry
        sweep(0, qi * sub - 1 - step, False)
        return step + 1, alive()

    lax.while_loop(cond, body, (jnp.int32(0), alive()))
    o_ref[...] = acc_ref[...].astype(o_ref.dtype)


def _stick_breaking(proj, *, batch, seq, heads, sub, q_col, k_col, v_col):
    n_tiles = seq // SB_TILE
    assert n_tiles % sub == 0
    tq = sub * SB_TILE
    return pl.pallas_call(
        functools.partial(_sb_kernel, sub=sub),
        grid=(batch, heads, n_tiles // sub),
        in_specs=[
            pl.BlockSpec((None, tq, HEAD_DIM), lambda b, h, i: (b, i, q_col + h)),
            pl.BlockSpec((None, seq, HEAD_DIM), lambda b, h, i: (b, 0, k_col + h)),
            pl.BlockSpec((None, seq, HEAD_DIM), lambda b, h, i: (b, 0, v_col + h)),
        ],
        out_specs=pl.BlockSpec((None, tq, HEAD_DIM), lambda b, h, i: (b, i, h)),
        out_shape=jax.ShapeDtypeStruct((batch, seq, heads * HEAD_DIM), BF16),
        scratch_shapes=[pltpu.VMEM((tq, 1), F32), pltpu.VMEM((tq, HEAD_DIM), F32)],
        compiler_params=_params("parallel", "parallel", "arbitrary"),
        name="stick_breaking",
    )(proj, proj, proj)


def _mix_kernel(a_ref, b_ref, wa_ref, wb_ref, ga_ref, gb_ref, o_ref):
    ya = _dot(a_ref[...], wa_ref[...])
    yb = _dot(b_ref[...], wb_ref[...])
    ga = jax.nn.sigmoid(ga_ref[...].astype(F32))
    gb = jax.nn.sigmoid(gb_ref[...].astype(F32))
    o_ref[...] = (ga * ya + gb * yb).astype(o_ref.dtype)


def _mix(att_a, att_b, wa, wb, proj, *, ga_col, gb_col, tm, tn):
    n, ka = att_a.shape
    kb = att_b.shape[1]
    d = wa.shape[1]
    return pl.pallas_call(
        _mix_kernel,
        grid=(n // tm, d // tn),
        in_specs=[
            pl.BlockSpec((tm, ka), lambda i, j: (i, 0)),
            pl.BlockSpec((tm, kb), lambda i, j: (i, 0)),
            pl.BlockSpec((ka, tn), lambda i, j: (0, j)),
            pl.BlockSpec((kb, tn), lambda i, j: (0, j)),
            pl.BlockSpec((tm, tn), lambda i, j: (i, ga_col // tn + j)),
            pl.BlockSpec((tm, tn), lambda i, j: (i, gb_col // tn + j)),
        ],
        out_specs=pl.BlockSpec((tm, tn), lambda i, j: (i, j)),
        out_shape=jax.ShapeDtypeStruct((n, d), BF16),
        compiler_params=_params("parallel", "arbitrary"),
        name="mix",
    )(att_a, att_b, wa, wb, proj, proj)


def _matmul_residual_kernel(a_ref, w_ref, r_ref, o_ref):
    o_ref[...] = r_ref[...] + _dot(a_ref[...], w_ref[...])


def _matmul_residual(a, w, r, *, tm, tn):
    n, k = a.shape
    d = w.shape[1]
    return pl.pallas_call(
        _matmul_residual_kernel,
        grid=(n // tm, d // tn),
        in_specs=[
            pl.BlockSpec((tm, k), lambda i, j: (i, 0)),
            pl.BlockSpec((k, tn), lambda i, j: (0, j)),
            pl.BlockSpec((tm, tn), lambda i, j: (i, j)),
        ],
        out_specs=pl.BlockSpec((tm, tn), lambda i, j: (i, j)),
        out_shape=jax.ShapeDtypeStruct((n, d), F32),
        compiler_params=_params("parallel", "arbitrary"),
        name="matmul_residual",
    )(a, w, r)


def _ple_kernel(h_ref, hc_ref, p_ref, g_ref, wg_ref, wp_ref, fg_ref, o_ref, xn_ref, ssq_ref):
    j = pl.program_id(1)
    nj = pl.num_programs(1)
    tn = hc_ref.shape[1]

    @pl.when(j == 0)
    def _():
        _rmsnorm_rows(h_ref, g_ref, xn_ref)
        ssq_ref[...] = jnp.zeros_like(ssq_ref)

    gate = jax.nn.sigmoid(_dot(xn_ref[...], wg_ref[...]))
    emb = _dot(p_ref[...].astype(BF16), wp_ref[...])
    new = hc_ref[...] + gate * emb
    o_ref[:, pl.ds(pl.multiple_of(j * tn, tn), tn)] = new
    ssq_ref[...] += jnp.sum(new * new, axis=-1, keepdims=True)

    @pl.when(j == nj - 1)
    def _():
        rows = o_ref.shape[0]
        chunk = min(NORM_ROWS, rows)

        def body(r, carry):
            sl = pl.ds(pl.multiple_of(r * chunk, chunk), chunk)
            inv = lax.rsqrt(ssq_ref[sl, :] * (1.0 / o_ref.shape[1]) + RMS_EPS)
            o_ref[sl, :] = (o_ref[sl, :] * inv) * fg_ref[...]
            return carry

        lax.fori_loop(0, rows // chunk, body, 0)


def _ple(h, p, g, wg, wp, fg, *, tm, tn):
    n, d = h.shape
    pd = p.shape[1]
    return pl.pallas_call(
        _ple_kernel,
        grid=(n // tm, d // tn),
        in_specs=[
            _row_block(tm, d),
            pl.BlockSpec((tm, tn), lambda i, j: (i, j)),
            pl.BlockSpec((tm, pd), lambda i, j: (i, 0)),
            pl.BlockSpec((1, d), lambda i, j: (0, 0)),
            pl.BlockSpec((d, tn), lambda i, j: (0, j)),
            pl.BlockSpec((pd, tn), lambda i, j: (0, j)),
            pl.BlockSpec((1, d), lambda i, j: (0, 0)),
        ],
        out_specs=_row_block(tm, d),
        out_shape=jax.ShapeDtypeStruct((n, d), F32),
        scratch_shapes=[
            pltpu.VMEM((tm, d), BF16),
            pltpu.VMEM((tm, 1), F32),
        ],
        compiler_params=_params("parallel", "arbitrary"),
        name="ple_final_norm",
    )(h, h, p, g, wg, wp, fg)


def _tile(n, want):
    t = min(n, want)
    assert n % t == 0, (n, want)
    return t


def kernel(x, p, ffn1_norm, ffn1_w_gate, ffn1_w_up, ffn1_w_down, mix_norm, w_in, w_branch_moba, w_branch_sb, w_out, ffn2_norm, ffn2_w_gate, ffn2_w_up, ffn2_w_down, ple_norm, w_ple_gate, w_ple_proj, final_norm):
    batch, seq, d = x.shape
    depth = p.shape[0]
    n = batch * seq
    moba_w = w_branch_moba.shape[1]
    sb_w = w_branch_sb.shape[1]
    assert moba_w % HEAD_DIM == 0 and sb_w % HEAD_DIM == 0
    assert seq % MOBA_BLOCK == 0 and seq % SB_TILE == 0
    assert w_in.shape[2] == 3 * moba_w + 3 * sb_w + 2 * d
    moba_heads = moba_w // HEAD_DIM
    sb_heads = sb_w // HEAD_DIM
    slopes = jnp.asarray(2.0 ** (-8.0 * np.arange(1, moba_heads + 1) / moba_heads), dtype=F32)
    off = np.cumsum([0, moba_w, moba_w, moba_w, sb_w, sb_w, sb_w, d])
    col = [int(o) // HEAD_DIM for o in off]

    tm = _tile(n, 512)
    tm_wide = _tile(n, 1024)
    tn = _tile(d, 1024)
    tf = _tile(ffn1_w_gate.shape[2], 256)
    row = lambda v: v.reshape(1, -1)
    bf = lambda w: w.astype(BF16)

    sub = min(SUPER_TILES, seq // MOBA_BLOCK)
    assert depth == 1
    i = 0
    h = x.reshape(n, d)
    h = _ffn(h, row(ffn1_norm[i]), bf(ffn1_w_gate[i]), bf(ffn1_w_up[i]), bf(ffn1_w_down[i]), tm=tm_wide, tf=tf)
    proj = _norm_proj(h, row(mix_norm[i]), bf(w_in[i]), tm=tm_wide, tn=tn)
    proj3 = proj.reshape(batch, seq, proj.shape[1])
    att_a = _moba(proj3, slopes, batch=batch, seq=seq, heads=moba_heads, sub=sub,
                  q_col=col[0], k_col=col[1], v_col=col[2])
    att_b = _stick_breaking(proj3, batch=batch, seq=seq, heads=sb_heads, sub=min(SB_SUPER_TILES, seq // SB_TILE),
                            q_col=col[3], k_col=col[4], v_col=col[5])
    mixed = _mix(att_a.reshape(n, moba_w), att_b.reshape(n, sb_w), bf(w_branch_moba[i]), bf(w_branch_sb[i]),
                 proj, ga_col=int(off[6]), gb_col=int(off[7]), tm=tm_wide, tn=tn)
    h = _matmul_residual(mixed, bf(w_out[i]), h, tm=tm_wide, tn=tn)
    h = _ffn(h, row(ffn2_norm[i]), bf(ffn2_w_gate[i]), bf(ffn2_w_up[i]), bf(ffn2_w_down[i]), tm=tm_wide, tf=tf)
    h = _ple(h, p[i].reshape(n, -1), row(ple_norm[i]), bf(w_ple_gate[i]), bf(w_ple_proj[i]), row(final_norm),
             tm=tm_wide, tn=_tile(d, 512))
    return h.reshape(batch, seq, d)
```

```python
import functools

import numpy as np
import jax
import jax.numpy as jnp
from jax import lax
from jax.experimental import pallas as pl
from jax.experimental.pallas import tpu as pltpu

F32 = jnp.float32
BF16 = jnp.bfloat16

HEAD_DIM = 128
MOBA_BLOCK = 256
MOBA_TOPK = 3
SB_TILE = 256
RMS_EPS = 1e-6
LANES = 128
NORM_ROWS = 64
VMEM_LIMIT_BYTES = 60 * 1024 * 1024
SUPER_TILES = 4
SB_SUPER_TILES = 4
LOG2E = 1.4426950408889634
MASKED = -(2.0 ** 100)
SB_DEAD_BITS = 150.0


def _dot(a, b):
    return jnp.dot(a, b, preferred_element_type=F32)


def _dot_nt(a, b):
    return lax.dot_general(a, b, (((1,), (1,)), ((), ())), preferred_element_type=F32)


def _params(*semantics):
    return pltpu.CompilerParams(dimension_semantics=semantics, vmem_limit_bytes=VMEM_LIMIT_BYTES)


def _row_block(tm, d, buffers=1):
    return pl.BlockSpec((tm, d), lambda i, j: (i, 0), pipeline_mode=pl.Buffered(buffers))


def _rmsnorm_rows(x_ref, g_ref, out_ref):
    rows = x_ref.shape[0]
    chunk = min(NORM_ROWS, rows)

    def body(r, carry):
        sl = pl.ds(pl.multiple_of(r * chunk, chunk), chunk)
        x = x_ref[sl, :]
        ms = jnp.mean(x * x, axis=-1, keepdims=True)
        out_ref[sl, :] = ((x * lax.rsqrt(ms + RMS_EPS)) * g_ref[...]).astype(out_ref.dtype)
        return carry

    lax.fori_loop(0, rows // chunk, body, 0)


def _ffn_kernel(x_ref, g_ref, wg_ref, wu_ref, wd_ref, o_ref, xn_ref):
    j = pl.program_id(1)

    @pl.when(j == 0)
    def _():
        _rmsnorm_rows(x_ref, g_ref, xn_ref)
        o_ref[...] = jnp.zeros_like(o_ref)

    xn = xn_ref[...]
    gate = _dot(xn, wg_ref[...])
    up = _dot(xn, wu_ref[...])
    act = (gate * jax.nn.sigmoid(gate) * up).astype(BF16)
    o_ref[...] += _dot(act, wd_ref[...])

    @pl.when(j == pl.num_programs(1) - 1)
    def _():
        o_ref[...] = x_ref[...] + 0.5 * o_ref[...]


def _ffn(x, g, wg, wu, wd, *, tm, tf, buffers=1):
    n, d = x.shape
    dff = wg.shape[1]
    return pl.pallas_call(
        _ffn_kernel,
        grid=(n // tm, dff // tf),
        in_specs=[
            _row_block(tm, d, buffers),
            pl.BlockSpec((1, d), lambda i, j: (0, 0)),
            pl.BlockSpec((d, tf), lambda i, j: (0, j)),
            pl.BlockSpec((d, tf), lambda i, j: (0, j)),
            pl.BlockSpec((tf, d), lambda i, j: (j, 0)),
        ],
        out_specs=_row_block(tm, d, buffers),
        out_shape=jax.ShapeDtypeStruct((n, d), F32),
        scratch_shapes=[pltpu.VMEM((tm, d), BF16)],
        compiler_params=_params("parallel", "arbitrary"),
        name="ffn",
    )(x, g, wg, wu, wd)


def _norm_proj_kernel(x_ref, g_ref, w_ref, o_ref, xn_ref):
    @pl.when(pl.program_id(1) == 0)
    def _():
        _rmsnorm_rows(x_ref, g_ref, xn_ref)

    o_ref[...] = _dot(xn_ref[...], w_ref[...]).astype(o_ref.dtype)


def _norm_proj(x, g, w, *, tm, tn, buffers=1):
    n, d = x.shape
    c = w.shape[1]
    return pl.pallas_call(
        _norm_proj_kernel,
        grid=(n // tm, c // tn),
        in_specs=[
            _row_block(tm, d, buffers),
            pl.BlockSpec((1, d), lambda i, j: (0, 0)),
            pl.BlockSpec((d, tn), lambda i, j: (0, j)),
        ],
        out_specs=pl.BlockSpec((tm, tn), lambda i, j: (i, j)),
        out_shape=jax.ShapeDtypeStruct((n, c), BF16),
        scratch_shapes=[pltpu.VMEM((tm, d), BF16)],
        compiler_params=_params("parallel", "arbitrary"),
        name="norm_proj",
    )(x, g, w)


def _moba_kernel(slope_ref, q_ref, k_ref, v_ref, o_ref, kaug_ref, vaug_ref, kmean_ref, qaug_ref, m_ref, acc_ref,
                 *, n_blocks, sub):
    h = pl.program_id(1)
    qi = pl.program_id(2)
    blk = MOBA_BLOCK
    dh = HEAD_DIM
    tq = sub * blk
    inv_scale = dh ** 0.5
    c2 = LOG2E / inv_scale
    bias_lane = LANES - 3

    @pl.when(qi == 0)
    def _():
        lane = lax.broadcasted_iota(jnp.int32, (blk, LANES), 1)
        pos = lax.broadcasted_iota(jnp.int32, (blk, LANES), 0)
        unit = slope_ref[h] * inv_scale
        kmean_ref[...] = jnp.zeros_like(kmean_ref)
        for n in range(n_blocks):
            rows = slice(n * blk, (n + 1) * blk)
            beta = (pos + n * blk).astype(F32) * unit
            b_hi = beta.astype(BF16).astype(F32)
            b_mid = (beta - b_hi).astype(BF16).astype(F32)
            b_lo = beta - b_hi - b_mid
            base = jnp.where(lane == bias_lane, b_hi,
                             jnp.where(lane == bias_lane + 1, b_mid,
                                       jnp.where(lane == bias_lane + 2, b_lo, 0.0)))
            kb = k_ref[rows, :]
            kaug_ref[rows, 0:dh] = kb
            kaug_ref[rows, dh:2 * dh] = jnp.where(lane == n, 1.0, base).astype(BF16)
            kmean_ref[n:n + 1, :] = jnp.sum(kb.astype(F32), axis=0, keepdims=True) * (1.0 / blk)
            vaug_ref[rows, 0:dh] = v_ref[rows, :]
            vaug_ref[rows, dh:2 * dh] = jnp.ones((blk, dh), BF16)

    q = q_ref[...]
    kmean = kmean_ref[...]
    km_hi = kmean.astype(BF16)
    km_lo = (kmean - km_hi.astype(F32)).astype(BF16)
    gate = _dot_nt(q, km_hi) + _dot_nt(q, km_lo)
    lane = lax.broadcasted_iota(jnp.int32, (tq, LANES), 1)
    own = qi * sub + lax.broadcasted_iota(jnp.int32, (tq, LANES), 0) // blk
    neg_inf = jnp.float32(-jnp.inf)
    gate = jnp.where(lane < own, gate, neg_inf)
    chosen = lane == own
    for _ in range(MOBA_TOPK):
        top = jnp.max(gate, axis=1, keepdims=True)
        first = jnp.min(jnp.where(gate == top, lane, LANES), axis=1, keepdims=True)
        pick = (lane == first) & (lane < own)
        chosen = chosen | pick
        gate = jnp.where(pick, neg_inf, gate)
    extra = jnp.where(lane >= bias_lane, 1.0, jnp.where(chosen, 0.0, MASKED))
    qaug_ref[:, 0:dh] = q
    qaug_ref[:, dh:2 * dh] = extra.astype(BF16)
    m_ref[...] = jnp.full_like(m_ref, MASKED)
    acc_ref[...] = jnp.zeros_like(acc_ref)

    def scores(row_lo, n, causal):
        start = pl.multiple_of(n * blk, blk)
        s = _dot_nt(qaug_ref[row_lo:tq, :], kaug_ref[pl.ds(start, blk), :])
        if causal:
            r = lax.broadcasted_iota(jnp.int32, s.shape, 0)
            c = lax.broadcasted_iota(jnp.int32, s.shape, 1)
            s = jnp.where(c <= r, s, MASKED)
        return s

    def max_pass(row_lo, n, causal):
        m_ref[row_lo:tq, :] = jnp.maximum(m_ref[row_lo:tq, :], scores(row_lo, n, causal))

    def sum_pass(row_lo, n, causal):
        start = pl.multiple_of(n * blk, blk)
        p = jnp.exp2(scores(row_lo, n, causal) * c2 - m_ref[row_lo:tq, :])
        acc_ref[row_lo:tq, :] += _dot(p.astype(BF16), vaug_ref[pl.ds(start, blk), :])

    def all_blocks(visit):
        for kk in range(sub):
            visit(kk * blk, qi * sub + kk, True)

        def body(i, carry):
            for u in range(sub):
                visit(0, i * sub + u, False)
            return carry

        lax.fori_loop(0, qi, body, 0)

    all_blocks(max_pass)
    m_ref[...] = jnp.broadcast_to(jnp.max(m_ref[...], axis=1, keepdims=True) * c2, m_ref.shape)
    all_blocks(sum_pass)
    acc = acc_ref[...]
    o_ref[...] = (acc[:, 0:dh] / acc[:, dh:2 * dh]).astype(o_ref.dtype)


def _moba(proj, slopes, *, batch, seq, heads, sub, q_col, k_col, v_col):
    n_blocks = seq // MOBA_BLOCK
    assert n_blocks <= LANES - 3 and n_blocks % sub == 0
    tq = sub * MOBA_BLOCK
    kernel = functools.partial(_moba_kernel, n_blocks=n_blocks, sub=sub)
    return pl.pallas_call(
        kernel,
        grid=(batch, heads, n_blocks // sub),
        in_specs=[
            pl.BlockSpec(memory_space=pltpu.SMEM),
            pl.BlockSpec((None, tq, HEAD_DIM), lambda b, h, i: (b, i, q_col + h)),
            pl.BlockSpec((None, seq, HEAD_DIM), lambda b, h, i: (b, 0, k_col + h)),
            pl.BlockSpec((None, seq, HEAD_DIM), lambda b, h, i: (b, 0, v_col + h)),
        ],
        out_specs=pl.BlockSpec((None, tq, HEAD_DIM), lambda b, h, i: (b, i, h)),
        out_shape=jax.ShapeDtypeStruct((batch, seq, heads * HEAD_DIM), BF16),
        scratch_shapes=[
            pltpu.VMEM((seq, 2 * HEAD_DIM), BF16),
            pltpu.VMEM((seq, 2 * HEAD_DIM), BF16),
            pltpu.VMEM((LANES, HEAD_DIM), F32),
            pltpu.VMEM((tq, 2 * HEAD_DIM), BF16),
            pltpu.VMEM((tq, MOBA_BLOCK), F32),
            pltpu.VMEM((tq, 2 * HEAD_DIM), F32),
        ],
        compiler_params=_params("parallel", "parallel", "arbitrary"),
        name="moba",
    )(slopes, proj, proj, proj)


def _sb_kernel(q_ref, k_ref, v_ref, o_ref, run_ref, acc_ref, *, sub):
    qi = pl.program_id(2)
    tile = SB_TILE
    tq = sub * tile
    c2 = LOG2E * HEAD_DIM ** -0.5
    row2 = lax.broadcasted_iota(jnp.int32, (2 * tile, tile), 0)
    col2 = lax.broadcasted_iota(jnp.int32, (2 * tile, tile), 1)
    suffix2 = ((row2 % tile) > col2).astype(BF16)
    sign_bit = jnp.uint32(0x80000000)
    run_ref[...] = jnp.zeros_like(run_ref)
    acc_ref[...] = jnp.zeros_like(acc_ref)

    def sweep(row_lo, row_hi, j, diagonal):
        rows = slice(row_lo, row_hi)
        start = pl.multiple_of(j * tile, tile)
        z2 = _dot_nt(q_ref[rows, :], k_ref[pl.ds(start, tile), :]) * c2
        neg_abs = lax.bitcast_convert_type(lax.bitcast_convert_type(z2, jnp.uint32) | sign_bit, F32)
        cost = jnp.maximum(z2, 0.0) + jnp.log(1.0 + jnp.exp2(neg_abs)) * LOG2E
        if diagonal:
            r = lax.broadcasted_iota(jnp.int32, z2.shape, 0)
            c = lax.broadcasted_iota(jnp.int32, z2.shape, 1)
            strict = c < r
            cost = jnp.where(strict, cost, 0.0)
        hi = cost.astype(BF16)
        lo = (cost - hi.astype(F32)).astype(BF16)
        later = _dot(jnp.concatenate([hi, lo], axis=1), suffix2)
        w = jnp.exp2(z2 - cost - later - run_ref[rows, :])
        if diagonal:
            w = jnp.where(strict, w, 0.0)
        acc_ref[rows, :] += _dot(w.astype(BF16), v_ref[pl.ds(start, tile), :])
        run_ref[rows, :] += jnp.sum(cost, axis=1, keepdims=True)

    def done_from(row):
        return jnp.min(run_ref[row:tq, :]) > SB_DEAD_BITS

    def guarded_sweep(row_lo, split, j, diagonal):
        if not row_lo < split < tq:
            sweep(row_lo, tq, j, diagonal)
            return
        tail_done = done_from(split)

        @pl.when(tail_done)
        def _():
            sweep(row_lo, split, j, diagonal)

        @pl.when(jnp.logical_not(tail_done))
        def _():
            sweep(row_lo, tq, j, diagonal)

    for kk in reversed(range(sub)):
        guarded_sweep(kk * tile, (kk + 2) * tile, qi * sub + kk, True)

    def cond(carry):
        step, go = carry
        return jnp.logical_and(step < qi * sub, go)

    def body(carry):
        step, _ = carry
        guarded_sweep(0, tile, qi * sub - 1 - step, False)
        return step + 1, jnp.logical_not(done_from(0))

    lax.while_loop(cond, body, (jnp.int32(0), jnp.logical_not(done_from(0))))
    o_ref[...] = acc_ref[...].astype(o_ref.dtype)


def _stick_breaking(proj, *, batch, seq, heads, sub, q_col, k_col, v_col):
    n_tiles = seq // SB_TILE
    assert n_tiles % sub == 0
    tq = sub * SB_TILE
    return pl.pallas_call(
        functools.partial(_sb_kernel, sub=sub),
        grid=(batch, heads, n_tiles // sub),
        in_specs=[
            pl.BlockSpec((None, tq, HEAD_DIM), lambda b, h, i: (b, i, q_col + h)),
            pl.BlockSpec((None, seq, HEAD_DIM), lambda b, h, i: (b, 0, k_col + h)),
            pl.BlockSpec((None, seq, HEAD_DIM), lambda b, h, i: (b, 0, v_col + h)),
        ],
        out_specs=pl.BlockSpec((None, tq, HEAD_DIM), lambda b, h, i: (b, i, h)),
        out_shape=jax.ShapeDtypeStruct((batch, seq, heads * HEAD_DIM), BF16),
        scratch_shapes=[pltpu.VMEM((tq, 1), F32), pltpu.VMEM((tq, HEAD_DIM), F32)],
        compiler_params=_params("parallel", "parallel", "arbitrary"),
        name="stick_breaking",
    )(proj, proj, proj)


def _mix_kernel(a_ref, b_ref, wa_ref, wb_ref, ga_ref, gb_ref, o_ref):
    ya = _dot(a_ref[...], wa_ref[...])
    yb = _dot(b_ref[...], wb_ref[...])
    ga = jax.nn.sigmoid(ga_ref[...].astype(F32))
    gb = jax.nn.sigmoid(gb_ref[...].astype(F32))
    o_ref[...] = (ga * ya + gb * yb).astype(o_ref.dtype)


def _mix(att_a, att_b, wa, wb, proj, *, ga_col, gb_col, tm, tn):
    n, ka = att_a.shape
    kb = att_b.shape[1]
    d = wa.shape[1]
    return pl.pallas_call(
        _mix_kernel,
        grid=(n // tm, d // tn),
        in_specs=[
            pl.BlockSpec((tm, ka), lambda i, j: (i, 0)),
            pl.BlockSpec((tm, kb), lambda i, j: (i, 0)),
            pl.BlockSpec((ka, tn), lambda i, j: (0, j)),
            pl.BlockSpec((kb, tn), lambda i, j: (0, j)),
            pl.BlockSpec((tm, tn), lambda i, j: (i, ga_col // tn + j)),
            pl.BlockSpec((tm, tn), lambda i, j: (i, gb_col // tn + j)),
        ],
        out_specs=pl.BlockSpec((tm, tn), lambda i, j: (i, j)),
        out_shape=jax.ShapeDtypeStruct((n, d), BF16),
        compiler_params=_params("parallel", "arbitrary"),
        name="mix",
    )(att_a, att_b, wa, wb, proj, proj)


def _matmul_residual_kernel(a_ref, w_ref, r_ref, o_ref):
    o_ref[...] = r_ref[...] + _dot(a_ref[...], w_ref[...])


def _matmul_residual(a, w, r, *, tm, tn):
    n, k = a.shape
    d = w.shape[1]
    return pl.pallas_call(
        _matmul_residual_kernel,
        grid=(n // tm, d // tn),
        in_specs=[
            pl.BlockSpec((tm, k), lambda i, j: (i, 0)),
            pl.BlockSpec((k, tn), lambda i, j: (0, j)),
            pl.BlockSpec((tm, tn), lambda i, j: (i, j)),
        ],
        out_specs=pl.BlockSpec((tm, tn), lambda i, j: (i, j)),
        out_shape=jax.ShapeDtypeStruct((n, d), F32),
        compiler_params=_params("parallel", "arbitrary"),
        name="matmul_residual",
    )(a, w, r)


def _ple_kernel(h_ref, hc_ref, p_ref, g_ref, wg_ref, wp_ref, fg_ref, o_ref, xn_ref, ssq_ref):
    j = pl.program_id(1)
    nj = pl.num_programs(1)
    tn = hc_ref.shape[1]

    @pl.when(j == 0)
    def _():
        _rmsnorm_rows(h_ref, g_ref, xn_ref)
        ssq_ref[...] = jnp.zeros_like(ssq_ref)

    gate = jax.nn.sigmoid(_dot(xn_ref[...], wg_ref[...]))
    emb = _dot(p_ref[...].astype(BF16), wp_ref[...])
    new = hc_ref[...] + gate * emb
    o_ref[:, pl.ds(pl.multiple_of(j * tn, tn), tn)] = new
    ssq_ref[...] += jnp.sum(new * new, axis=-1, keepdims=True)

    @pl.when(j == nj - 1)
    def _():
        rows = o_ref.shape[0]
        chunk = min(NORM_ROWS, rows)

        def body(r, carry):
            sl = pl.ds(pl.multiple_of(r * chunk, chunk), chunk)
            inv = lax.rsqrt(ssq_ref[sl, :] * (1.0 / o_ref.shape[1]) + RMS_EPS)
            o_ref[sl, :] = (o_ref[sl, :] * inv) * fg_ref[...]
            return carry

        lax.fori_loop(0, rows // chunk, body, 0)


def _ple(h, p, g, wg, wp, fg, *, tm, tn):
    n, d = h.shape
    pd = p.shape[1]
    return pl.pallas_call(
        _ple_kernel,
        grid=(n // tm, d // tn),
        in_specs=[
            _row_block(tm, d, buffers=2),
            pl.BlockSpec((tm, tn), lambda i, j: (i, j)),
            pl.BlockSpec((tm, pd), lambda i, j: (i, 0)),
            pl.BlockSpec((1, d), lambda i, j: (0, 0)),
            pl.BlockSpec((d, tn), lambda i, j: (0, j)),
            pl.BlockSpec((pd, tn), lambda i, j: (0, j)),
            pl.BlockSpec((1, d), lambda i, j: (0, 0)),
        ],
        out_specs=_row_block(tm, d, buffers=2),
        out_shape=jax.ShapeDtypeStruct((n, d), F32),
        scratch_shapes=[
            pltpu.VMEM((tm, d), BF16),
            pltpu.VMEM((tm, 1), F32),
        ],
        compiler_params=_params("parallel", "arbitrary"),
        name="ple_final_norm",
    )(h, h, p, g, wg, wp, fg)


def _tile(n, want):
    t = min(n, want)
    assert n % t == 0, (n, want)
    return t


def kernel(x, p, ffn1_norm, ffn1_w_gate, ffn1_w_up, ffn1_w_down, mix_norm, w_in, w_branch_moba, w_branch_sb, w_out, ffn2_norm, ffn2_w_gate, ffn2_w_up, ffn2_w_down, ple_norm, w_ple_gate, w_ple_proj, final_norm):
    batch, seq, d = x.shape
    depth = p.shape[0]
    n = batch * seq
    moba_w = w_branch_moba.shape[1]
    sb_w = w_branch_sb.shape[1]
    assert moba_w % HEAD_DIM == 0 and sb_w % HEAD_DIM == 0
    assert seq % MOBA_BLOCK == 0 and seq % SB_TILE == 0
    assert w_in.shape[2] == 3 * moba_w + 3 * sb_w + 2 * d
    moba_heads = moba_w // HEAD_DIM
    sb_heads = sb_w // HEAD_DIM
    slopes = jnp.asarray(2.0 ** (-8.0 * np.arange(1, moba_heads + 1) / moba_heads), dtype=F32)
    off = np.cumsum([0, moba_w, moba_w, moba_w, sb_w, sb_w, sb_w, d])
    col = [int(o) // HEAD_DIM for o in off]

    tm = _tile(n, 512)
    tm_wide = _tile(n, 1024)
    tn = _tile(d, 1024)
    tf = _tile(ffn1_w_gate.shape[2], 256)
    row = lambda v: v.reshape(1, -1)
    bf = lambda w: w.astype(BF16)

    sub = min(SUPER_TILES, seq // MOBA_BLOCK)
    assert depth == 1
    i = 0
    h = x.reshape(n, d)
    h = _ffn(h, row(ffn1_norm[i]), bf(ffn1_w_gate[i]), bf(ffn1_w_up[i]), bf(ffn1_w_down[i]), tm=tm_wide, tf=tf)
    proj = _norm_proj(h, row(mix_norm[i]), bf(w_in[i]), tm=tm_wide, tn=tn)
    proj3 = proj.reshape(batch, seq, proj.shape[1])
    att_a = _moba(proj3, slopes, batch=batch, seq=seq, heads=moba_heads, sub=sub,
                  q_col=col[0], k_col=col[1], v_col=col[2])
    att_b = _stick_breaking(proj3, batch=batch, seq=seq, heads=sb_heads, sub=min(SB_SUPER_TILES, seq // SB_TILE),
                            q_col=col[3], k_col=col[4], v_col=col[5])
    mixed = _mix(att_a.reshape(n, moba_w), att_b.reshape(n, sb_w), bf(w_branch_moba[i]), bf(w_branch_sb[i]),
                 proj, ga_col=int(off[6]), gb_col=int(off[7]), tm=tm_wide, tn=tn)
    h = _matmul_residual(mixed, bf(w_out[i]), h, tm=tm_wide, tn=tn)
    h = _ffn(h, row(ffn2_norm[i]), bf(ffn2_w_gate[i]), bf(ffn2_w_up[i]), bf(ffn2_w_down[i]), tm=tm_wide, tf=tf)
    h = _ple(h, p[i].reshape(n, -1), row(ple_norm[i]), bf(w_ple_gate[i]), bf(w_ple_proj[i]), row(final_norm),
             tm=tm, tn=_tile(d, 512))
    return h.reshape(batch, seq, d)
```

```python
import functools

import numpy as np
import jax
import jax.numpy as jnp
from jax import lax
from jax.experimental import pallas as pl
from jax.experimental.pallas import tpu as pltpu

F32 = jnp.float32
BF16 = jnp.bfloat16

HEAD_DIM = 128
MOBA_BLOCK = 256
MOBA_TOPK = 3
SB_TILE = 256
RMS_EPS = 1e-6
LANES = 128
NORM_ROWS = 64
VMEM_LIMIT_BYTES = 60 * 1024 * 1024
SUPER_TILES = 4
SB_SUPER_TILES = 4
LOG2E = 1.4426950408889634
MASKED = -(2.0 ** 100)
SB_DEAD_BITS = 150.0


def _dot(a, b):
    return jnp.dot(a, b, preferred_element_type=F32)


def _dot_nt(a, b):
    return lax.dot_general(a, b, (((1,), (1,)), ((), ())), preferred_element_type=F32)


def _params(*semantics):
    return pltpu.CompilerParams(dimension_semantics=semantics, vmem_limit_bytes=VMEM_LIMIT_BYTES)


def _row_block(tm, d, buffers=1):
    return pl.BlockSpec((tm, d), lambda i, j: (i, 0), pipeline_mode=pl.Buffered(buffers))


def _rmsnorm_rows(x_ref, g_ref, out_ref):
    rows = x_ref.shape[0]
    chunk = min(NORM_ROWS, rows)

    def body(r, carry):
        sl = pl.ds(pl.multiple_of(r * chunk, chunk), chunk)
        x = x_ref[sl, :]
        ms = jnp.mean(x * x, axis=-1, keepdims=True)
        out_ref[sl, :] = ((x * lax.rsqrt(ms + RMS_EPS)) * g_ref[...]).astype(out_ref.dtype)
        return carry

    lax.fori_loop(0, rows // chunk, body, 0)


def _ffn_kernel(x_ref, g_ref, wg_ref, wu_ref, wd_ref, o_ref, xn_ref):
    j = pl.program_id(1)

    @pl.when(j == 0)
    def _():
        _rmsnorm_rows(x_ref, g_ref, xn_ref)
        o_ref[...] = jnp.zeros_like(o_ref)

    xn = xn_ref[...]
    gate = _dot(xn, wg_ref[...])
    up = _dot(xn, wu_ref[...])
    act = (gate * jax.nn.sigmoid(gate) * up).astype(BF16)
    o_ref[...] += _dot(act, wd_ref[...])

    @pl.when(j == pl.num_programs(1) - 1)
    def _():
        o_ref[...] = x_ref[...] + 0.5 * o_ref[...]


def _ffn(x, g, wg, wu, wd, *, tm, tf, buffers=1):
    n, d = x.shape
    dff = wg.shape[1]
    return pl.pallas_call(
        _ffn_kernel,
        grid=(n // tm, dff // tf),
        in_specs=[
            _row_block(tm, d, buffers),
            pl.BlockSpec((1, d), lambda i, j: (0, 0)),
            pl.BlockSpec((d, tf), lambda i, j: (0, j)),
            pl.BlockSpec((d, tf), lambda i, j: (0, j)),
            pl.BlockSpec((tf, d), lambda i, j: (j, 0)),
        ],
        out_specs=_row_block(tm, d, buffers),
        out_shape=jax.ShapeDtypeStruct((n, d), F32),
        scratch_shapes=[pltpu.VMEM((tm, d), BF16)],
        compiler_params=_params("parallel", "arbitrary"),
        name="ffn",
    )(x, g, wg, wu, wd)


def _norm_proj_kernel(x_ref, g_ref, w_ref, o_ref, xn_ref):
    @pl.when(pl.program_id(1) == 0)
    def _():
        _rmsnorm_rows(x_ref, g_ref, xn_ref)

    o_ref[...] = _dot(xn_ref[...], w_ref[...]).astype(o_ref.dtype)


def _norm_proj(x, g, w, *, tm, tn, buffers=1):
    n, d = x.shape
    c = w.shape[1]
    return pl.pallas_call(
        _norm_proj_kernel,
        grid=(n // tm, c // tn),
        in_specs=[
            _row_block(tm, d, buffers),
            pl.BlockSpec((1, d), lambda i, j: (0, 0)),
            pl.BlockSpec((d, tn), lambda i, j: (0, j)),
        ],
        out_specs=pl.BlockSpec((tm, tn), lambda i, j: (i, j)),
        out_shape=jax.ShapeDtypeStruct((n, c), BF16),
        scratch_shapes=[pltpu.VMEM((tm, d), BF16)],
        compiler_params=_params("parallel", "arbitrary"),
        name="norm_proj",
    )(x, g, w)


def _moba_kernel(slope_ref, q_ref, k_ref, v_ref, o_ref, kaug_ref, vaug_ref, kmean_ref, qaug_ref, m_ref, acc_ref,
                 *, n_blocks, sub):
    h = pl.program_id(1)
    qi = pl.program_id(2)
    blk = MOBA_BLOCK
    dh = HEAD_DIM
    tq = sub * blk
    inv_scale = dh ** 0.5
    c2 = LOG2E / inv_scale
    bias_lane = LANES - 3

    @pl.when(qi == 0)
    def _():
        lane = lax.broadcasted_iota(jnp.int32, (blk, LANES), 1)
        pos = lax.broadcasted_iota(jnp.int32, (blk, LANES), 0)
        unit = slope_ref[h] * inv_scale
        kmean_ref[...] = jnp.zeros_like(kmean_ref)
        for n in range(n_blocks):
            rows = slice(n * blk, (n + 1) * blk)
            beta = (pos + n * blk).astype(F32) * unit
            b_hi = beta.astype(BF16).astype(F32)
            b_mid = (beta - b_hi).astype(BF16).astype(F32)
            b_lo = beta - b_hi - b_mid
            base = jnp.where(lane == bias_lane, b_hi,
                             jnp.where(lane == bias_lane + 1, b_mid,
                                       jnp.where(lane == bias_lane + 2, b_lo, 0.0)))
            kb = k_ref[rows, :]
            kaug_ref[rows, 0:dh] = kb
            kaug_ref[rows, dh:2 * dh] = jnp.where(lane == n, 1.0, base).astype(BF16)
            kmean_ref[n:n + 1, :] = jnp.sum(kb.astype(F32), axis=0, keepdims=True) * (1.0 / blk)
            vaug_ref[rows, 0:dh] = v_ref[rows, :]
            vaug_ref[rows, dh:2 * dh] = jnp.ones((blk, dh), BF16)

    q = q_ref[...]
    kmean = kmean_ref[...]
    km_hi = kmean.astype(BF16)
    km_lo = (kmean - km_hi.astype(F32)).astype(BF16)
    gate = _dot_nt(km_hi, q) + _dot_nt(km_lo, q)
    block = lax.broadcasted_iota(jnp.int32, (LANES, tq), 0)
    own = qi * sub + lax.broadcasted_iota(jnp.int32, (LANES, tq), 1) // blk
    neg_inf = jnp.float32(-jnp.inf)
    gate = jnp.where(block < own, gate, neg_inf)
    chosen = block == own
    for _ in range(MOBA_TOPK):
        top = jnp.max(gate, axis=0, keepdims=True)
        first = jnp.min(jnp.where(gate == top, block, LANES), axis=0, keepdims=True)
        pick = (block == first) & (block < own)
        chosen = chosen | pick
        gate = jnp.where(pick, neg_inf, gate)
    chosen_rows = jnp.transpose(chosen.astype(F32))
    lane = lax.broadcasted_iota(jnp.int32, (tq, LANES), 1)
    extra = jnp.where(lane >= bias_lane, 1.0, jnp.where(chosen_rows > 0.5, 0.0, MASKED))
    qaug_ref[:, 0:dh] = q
    qaug_ref[:, dh:2 * dh] = extra.astype(BF16)
    m_ref[...] = jnp.full_like(m_ref, MASKED)
    acc_ref[...] = jnp.zeros_like(acc_ref)

    def scores(row_lo, n, causal):
        start = pl.multiple_of(n * blk, blk)
        s = _dot_nt(qaug_ref[row_lo:tq, :], kaug_ref[pl.ds(start, blk), :])
        if causal:
            r = lax.broadcasted_iota(jnp.int32, s.shape, 0)
            c = lax.broadcasted_iota(jnp.int32, s.shape, 1)
            s = jnp.where(c <= r, s, MASKED)
        return s

    def max_pass(row_lo, n, causal):
        m_ref[row_lo:tq, :] = jnp.maximum(m_ref[row_lo:tq, :], scores(row_lo, n, causal))

    def sum_pass(row_lo, n, causal):
        start = pl.multiple_of(n * blk, blk)
        p = jnp.exp2(scores(row_lo, n, causal) * c2 - m_ref[row_lo:tq, :])
        acc_ref[row_lo:tq, :] += _dot(p.astype(BF16), vaug_ref[pl.ds(start, blk), :])

    def all_blocks(visit):
        for kk in range(sub):
            visit(kk * blk, qi * sub + kk, True)

        def body(i, carry):
            for u in range(sub):
                visit(0, i * sub + u, False)
            return carry

        lax.fori_loop(0, qi, body, 0)

    all_blocks(max_pass)
    m_ref[...] = jnp.broadcast_to(jnp.max(m_ref[...], axis=1, keepdims=True) * c2, m_ref.shape)
    all_blocks(sum_pass)
    acc = acc_ref[...]
    o_ref[...] = (acc[:, 0:dh] / acc[:, dh:2 * dh]).astype(o_ref.dtype)


def _moba(proj, slopes, *, batch, seq, heads, sub, q_col, k_col, v_col):
    n_blocks = seq // MOBA_BLOCK
    assert n_blocks <= LANES - 3 and n_blocks % sub == 0
    tq = sub * MOBA_BLOCK
    kernel = functools.partial(_moba_kernel, n_blocks=n_blocks, sub=sub)
    return pl.pallas_call(
        kernel,
        grid=(batch, heads, n_blocks // sub),
        in_specs=[
            pl.BlockSpec(memory_space=pltpu.SMEM),
            pl.BlockSpec((None, tq, HEAD_DIM), lambda b, h, i: (b, i, q_col + h)),
            pl.BlockSpec((None, seq, HEAD_DIM), lambda b, h, i: (b, 0, k_col + h)),
            pl.BlockSpec((None, seq, HEAD_DIM), lambda b, h, i: (b, 0, v_col + h)),
        ],
        out_specs=pl.BlockSpec((None, tq, HEAD_DIM), lambda b, h, i: (b, i, h)),
        out_shape=jax.ShapeDtypeStruct((batch, seq, heads * HEAD_DIM), BF16),
        scratch_shapes=[
            pltpu.VMEM((seq, 2 * HEAD_DIM), BF16),
            pltpu.VMEM((seq, 2 * HEAD_DIM), BF16),
            pltpu.VMEM((LANES, HEAD_DIM), F32),
            pltpu.VMEM((tq, 2 * HEAD_DIM), BF16),
            pltpu.VMEM((tq, MOBA_BLOCK), F32),
            pltpu.VMEM((tq, 2 * HEAD_DIM), F32),
        ],
        compiler_params=_params("parallel", "parallel", "arbitrary"),
        name="moba",
    )(slopes, proj, proj, proj)


def _sb_kernel(q_ref, k_ref, v_ref, o_ref, run_ref, acc_ref, *, sub):
    qi = pl.program_id(2)
    tile = SB_TILE
    tq = sub * tile
    c2 = LOG2E * HEAD_DIM ** -0.5
    row2 = lax.broadcasted_iota(jnp.int32, (2 * tile, tile), 0)
    col2 = lax.broadcasted_iota(jnp.int32, (2 * tile, tile), 1)
    suffix2 = ((row2 % tile) > col2).astype(BF16)
    sign_bit = jnp.uint32(0x80000000)
    run_ref[...] = jnp.zeros_like(run_ref)
    acc_ref[...] = jnp.zeros_like(acc_ref)

    def sweep(row_lo, j, diagonal):
        rows = slice(row_lo, tq)
        start = pl.multiple_of(j * tile, tile)
        z2 = _dot_nt(q_ref[rows, :], k_ref[pl.ds(start, tile), :]) * c2
        neg_abs = lax.bitcast_convert_type(lax.bitcast_convert_type(z2, jnp.uint32) | sign_bit, F32)
        cost = jnp.maximum(z2, 0.0) + jnp.log(1.0 + jnp.exp2(neg_abs)) * LOG2E
        if diagonal:
            r = lax.broadcasted_iota(jnp.int32, z2.shape, 0)
            c = lax.broadcasted_iota(jnp.int32, z2.shape, 1)
            strict = c < r
            cost = jnp.where(strict, cost, 0.0)
        hi = cost.astype(BF16)
        lo = (cost - hi.astype(F32)).astype(BF16)
        later = _dot(jnp.concatenate([hi, lo], axis=1), suffix2)
        w = jnp.exp2(z2 - cost - later - run_ref[rows, :])
        if diagonal:
            w = jnp.where(strict, w, 0.0)
        acc_ref[rows, :] += _dot(w.astype(BF16), v_ref[pl.ds(start, tile), :])
        run_ref[rows, :] += jnp.sum(cost, axis=1, keepdims=True)

    for kk in reversed(range(sub)):
        sweep(kk * tile, qi * sub + kk, True)

    def alive():
        return jnp.min(run_ref[...]) <= SB_DEAD_BITS

    def cond(carry):
        step, go = carry
        return jnp.logical_and(step < qi * sub, go)

    def body(carry):
        step, _ = carry
        sweep(0, qi * sub - 1 - step, False)
        return step + 1, alive()

    lax.while_loop(cond, body, (jnp.int32(0), alive()))
    o_ref[...] = acc_ref[...].astype(o_ref.dtype)


def _stick_breaking(proj, *, batch, seq, heads, sub, q_col, k_col, v_col):
    n_tiles = seq // SB_TILE
    assert n_tiles % sub == 0
    tq = sub * SB_TILE
    return pl.pallas_call(
        functools.partial(_sb_kernel, sub=sub),
        grid=(batch, heads, n_tiles // sub),
        in_specs=[
            pl.BlockSpec((None, tq, HEAD_DIM), lambda b, h, i: (b, i, q_col + h)),
            pl.BlockSpec((None, seq, HEAD_DIM), lambda b, h, i: (b, 0, k_col + h)),
            pl.BlockSpec((None, seq, HEAD_DIM), lambda b, h, i: (b, 0, v_col + h)),
        ],
        out_specs=pl.BlockSpec((None, tq, HEAD_DIM), lambda b, h, i: (b, i, h)),
        out_shape=jax.ShapeDtypeStruct((batch, seq, heads * HEAD_DIM), BF16),
        scratch_shapes=[pltpu.VMEM((tq, 1), F32), pltpu.VMEM((tq, HEAD_DIM), F32)],
        compiler_params=_params("parallel", "parallel", "arbitrary"),
        name="stick_breaking",
    )(proj, proj, proj)


def _mix_kernel(a_ref, b_ref, wa_ref, wb_ref, ga_ref, gb_ref, o_ref):
    ya = _dot(a_ref[...], wa_ref[...])
    yb = _dot(b_ref[...], wb_ref[...])
    ga = jax.nn.sigmoid(ga_ref[...].astype(F32))
    gb = jax.nn.sigmoid(gb_ref[...].astype(F32))
    o_ref[...] = (ga * ya + gb * yb).astype(o_ref.dtype)


def _mix(att_a, att_b, wa, wb, proj, *, ga_col, gb_col, tm, tn):
    n, ka = att_a.shape
    kb = att_b.shape[1]
    d = wa.shape[1]
    return pl.pallas_call(
        _mix_kernel,
        grid=(n // tm, d // tn),
        in_specs=[
            pl.BlockSpec((tm, ka), lambda i, j: (i, 0)),
            pl.BlockSpec((tm, kb), lambda i, j: (i, 0)),
            pl.BlockSpec((ka, tn), lambda i, j: (0, j)),
            pl.BlockSpec((kb, tn), lambda i, j: (0, j)),
            pl.BlockSpec((tm, tn), lambda i, j: (i, ga_col // tn + j)),
            pl.BlockSpec((tm, tn), lambda i, j: (i, gb_col // tn + j)),
        ],
        out_specs=pl.BlockSpec((tm, tn), lambda i, j: (i, j)),
        out_shape=jax.ShapeDtypeStruct((n, d), BF16),
        compiler_params=_params("parallel", "arbitrary"),
        name="mix",
    )(att_a, att_b, wa, wb, proj, proj)


def _matmul_residual_kernel(a_ref, w_ref, r_ref, o_ref):
    o_ref[...] = r_ref[...] + _dot(a_ref[...], w_ref[...])


def _matmul_residual(a, w, r, *, tm, tn):
    n, k = a.shape
    d = w.shape[1]
    return pl.pallas_call(
        _matmul_residual_kernel,
        grid=(n // tm, d // tn),
        in_specs=[
            pl.BlockSpec((tm, k), lambda i, j: (i, 0)),
            pl.BlockSpec((k, tn), lambda i, j: (0, j)),
            pl.BlockSpec((tm, tn), lambda i, j: (i, j)),
        ],
        out_specs=pl.BlockSpec((tm, tn), lambda i, j: (i, j)),
        out_shape=jax.ShapeDtypeStruct((n, d), F32),
        compiler_params=_params("parallel", "arbitrary"),
        name="matmul_residual",
    )(a, w, r)


def _ple_kernel(h_ref, hc_ref, p_ref, g_ref, wg_ref, wp_ref, fg_ref, o_ref, xn_ref, ssq_ref):
    j = pl.program_id(1)
    nj = pl.num_programs(1)
    tn = hc_ref.shape[1]

    @pl.when(j == 0)
    def _():
        _rmsnorm_rows(h_ref, g_ref, xn_ref)
        ssq_ref[...] = jnp.zeros_like(ssq_ref)

    gate = jax.nn.sigmoid(_dot(xn_ref[...], wg_ref[...]))
    emb = _dot(p_ref[...].astype(BF16), wp_ref[...])
    new = hc_ref[...] + gate * emb
    o_ref[:, pl.ds(pl.multiple_of(j * tn, tn), tn)] = new
    ssq_ref[...] += jnp.sum(new * new, axis=-1, keepdims=True)

    @pl.when(j == nj - 1)
    def _():
        rows = o_ref.shape[0]
        chunk = min(NORM_ROWS, rows)

        def body(r, carry):
            sl = pl.ds(pl.multiple_of(r * chunk, chunk), chunk)
            inv = lax.rsqrt(ssq_ref[sl, :] * (1.0 / o_ref.shape[1]) + RMS_EPS)
            o_ref[sl, :] = (o_ref[sl, :] * inv) * fg_ref[...]
            return carry

        lax.fori_loop(0, rows // chunk, body, 0)


def _ple(h, p, g, wg, wp, fg, *, tm, tn):
    n, d = h.shape
    pd = p.shape[1]
    return pl.pallas_call(
        _ple_kernel,
        grid=(n // tm, d // tn),
        in_specs=[
            _row_block(tm, d, buffers=2),
            pl.BlockSpec((tm, tn), lambda i, j: (i, j)),
            pl.BlockSpec((tm, pd), lambda i, j: (i, 0)),
            pl.BlockSpec((1, d), lambda i, j: (0, 0)),
            pl.BlockSpec((d, tn), lambda i, j: (0, j)),
            pl.BlockSpec((pd, tn), lambda i, j: (0, j)),
            pl.BlockSpec((1, d), lambda i, j: (0, 0)),
        ],
        out_specs=_row_block(tm, d, buffers=2),
        out_shape=jax.ShapeDtypeStruct((n, d), F32),
        scratch_shapes=[
            pltpu.VMEM((tm, d), BF16),
            pltpu.VMEM((tm, 1), F32),
        ],
        compiler_params=_params("parallel", "arbitrary"),
        name="ple_final_norm",
    )(h, h, p, g, wg, wp, fg)


def _tile(n, want):
    t = min(n, want)
    assert n % t == 0, (n, want)
    return t


def kernel(x, p, ffn1_norm, ffn1_w_gate, ffn1_w_up, ffn1_w_down, mix_norm, w_in, w_branch_moba, w_branch_sb, w_out, ffn2_norm, ffn2_w_gate, ffn2_w_up, ffn2_w_down, ple_norm, w_ple_gate, w_ple_proj, final_norm):
    batch, seq, d = x.shape
    depth = p.shape[0]
    n = batch * seq
    moba_w = w_branch_moba.shape[1]
    sb_w = w_branch_sb.shape[1]
    assert moba_w % HEAD_DIM == 0 and sb_w % HEAD_DIM == 0
    assert seq % MOBA_BLOCK == 0 and seq % SB_TILE == 0
    assert w_in.shape[2] == 3 * moba_w + 3 * sb_w + 2 * d
    moba_heads = moba_w // HEAD_DIM
    sb_heads = sb_w // HEAD_DIM
    slopes = jnp.asarray(2.0 ** (-8.0 * np.arange(1, moba_heads + 1) / moba_heads), dtype=F32)
    off = np.cumsum([0, moba_w, moba_w, moba_w, sb_w, sb_w, sb_w, d])
    col = [int(o) // HEAD_DIM for o in off]

    tm = _tile(n, 512)
    tm_wide = _tile(n, 1024)
    tn = _tile(d, 1024)
    tf = _tile(ffn1_w_gate.shape[2], 256)
    row = lambda v: v.reshape(1, -1)
    bf = lambda w: w.astype(BF16)

    sub = min(SUPER_TILES, seq // MOBA_BLOCK)
    assert depth == 1
    i = 0
    h = x.reshape(n, d)
    h = _ffn(h, row(ffn1_norm[i]), bf(ffn1_w_gate[i]), bf(ffn1_w_up[i]), bf(ffn1_w_down[i]), tm=tm_wide, tf=tf)
    proj = _norm_proj(h, row(mix_norm[i]), bf(w_in[i]), tm=tm_wide, tn=tn)
    proj3 = proj.reshape(batch, seq, proj.shape[1])
    att_a = _moba(proj3, slopes, batch=batch, seq=seq, heads=moba_heads, sub=sub,
                  q_col=col[0], k_col=col[1], v_col=col[2])
    att_b = _stick_breaking(proj3, batch=batch, seq=seq, heads=sb_heads, sub=min(SB_SUPER_TILES, seq // SB_TILE),
                            q_col=col[3], k_col=col[4], v_col=col[5])
    mixed = _mix(att_a.reshape(n, moba_w), att_b.reshape(n, sb_w), bf(w_branch_moba[i]), bf(w_branch_sb[i]),
                 proj, ga_col=int(off[6]), gb_col=int(off[7]), tm=tm_wide, tn=tn)
    h = _matmul_residual(mixed, bf(w_out[i]), h, tm=tm_wide, tn=tn)
    h = _ffn(h, row(ffn2_norm[i]), bf(ffn2_w_gate[i]), bf(ffn2_w_up[i]), bf(ffn2_w_down[i]), tm=tm_wide, tf=tf)
    h = _ple(h, p[i].reshape(n, -1), row(ple_norm[i]), bf(w_ple_gate[i]), bf(w_ple_proj[i]), row(final_norm),
             tm=tm, tn=_tile(d, 512))
    return h.reshape(batch, seq, d)
```

```python
import functools

import numpy as np
import jax
import jax.numpy as jnp
from jax import lax
from jax.experimental import pallas as pl
from jax.experimental.pallas import tpu as pltpu

F32 = jnp.float32
BF16 = jnp.bfloat16

HEAD_DIM = 128
MOBA_BLOCK = 256
MOBA_TOPK = 3
SB_TILE = 256
RMS_EPS = 1e-6
LANES = 128
NORM_ROWS = 64
VMEM_LIMIT_BYTES = 60 * 1024 * 1024
SUPER_TILES = 4
SB_SUPER_TILES = 4
LOG2E = 1.4426950408889634
MASKED = -(2.0 ** 100)
SB_DEAD_BITS = 150.0


def _dot(a, b):
    return jnp.dot(a, b, preferred_element_type=F32)


def _dot_nt(a, b):
    return lax.dot_general(a, b, (((1,), (1,)), ((), ())), preferred_element_type=F32)


def _params(*semantics):
    return pltpu.CompilerParams(dimension_semantics=semantics, vmem_limit_bytes=VMEM_LIMIT_BYTES)


def _row_block(tm, d, buffers=1):
    return pl.BlockSpec((tm, d), lambda i, j: (i, 0), pipeline_mode=pl.Buffered(buffers))


def _rmsnorm_rows(x_ref, g_ref, out_ref):
    rows = x_ref.shape[0]
    chunk = min(NORM_ROWS, rows)

    def body(r, carry):
        sl = pl.ds(pl.multiple_of(r * chunk, chunk), chunk)
        x = x_ref[sl, :]
        ms = jnp.mean(x * x, axis=-1, keepdims=True)
        out_ref[sl, :] = ((x * lax.rsqrt(ms + RMS_EPS)) * g_ref[...]).astype(out_ref.dtype)
        return carry

    lax.fori_loop(0, rows // chunk, body, 0)


def _ffn_kernel(x_ref, g_ref, wg_ref, wu_ref, wd_ref, o_ref, xn_ref):
    j = pl.program_id(1)

    @pl.when(j == 0)
    def _():
        _rmsnorm_rows(x_ref, g_ref, xn_ref)
        o_ref[...] = jnp.zeros_like(o_ref)

    xn = xn_ref[...]
    gate = _dot(xn, wg_ref[...])
    up = _dot(xn, wu_ref[...])
    act = (gate * jax.nn.sigmoid(gate) * up).astype(BF16)
    o_ref[...] += _dot(act, wd_ref[...])

    @pl.when(j == pl.num_programs(1) - 1)
    def _():
        o_ref[...] = x_ref[...] + 0.5 * o_ref[...]


def _ffn(x, g, wg, wu, wd, *, tm, tf, buffers=1):
    n, d = x.shape
    dff = wg.shape[1]
    return pl.pallas_call(
        _ffn_kernel,
        grid=(n // tm, dff // tf),
        in_specs=[
            _row_block(tm, d, buffers),
            pl.BlockSpec((1, d), lambda i, j: (0, 0)),
            pl.BlockSpec((d, tf), lambda i, j: (0, j)),
            pl.BlockSpec((d, tf), lambda i, j: (0, j)),
            pl.BlockSpec((tf, d), lambda i, j: (j, 0)),
        ],
        out_specs=_row_block(tm, d, buffers),
        out_shape=jax.ShapeDtypeStruct((n, d), F32),
        scratch_shapes=[pltpu.VMEM((tm, d), BF16)],
        compiler_params=_params("parallel", "arbitrary"),
        name="ffn",
    )(x, g, wg, wu, wd)


def _norm_proj_kernel(x_ref, g_ref, w_ref, o_ref, xn_ref):
    @pl.when(pl.program_id(1) == 0)
    def _():
        _rmsnorm_rows(x_ref, g_ref, xn_ref)

    o_ref[...] = _dot(xn_ref[...], w_ref[...]).astype(o_ref.dtype)


def _norm_proj(x, g, w, *, tm, tn, buffers=1):
    n, d = x.shape
    c = w.shape[1]
    return pl.pallas_call(
        _norm_proj_kernel,
        grid=(n // tm, c // tn),
        in_specs=[
            _row_block(tm, d, buffers),
            pl.BlockSpec((1, d), lambda i, j: (0, 0)),
            pl.BlockSpec((d, tn), lambda i, j: (0, j)),
        ],
        out_specs=pl.BlockSpec((tm, tn), lambda i, j: (i, j)),
        out_shape=jax.ShapeDtypeStruct((n, c), BF16),
        scratch_shapes=[pltpu.VMEM((tm, d), BF16)],
        compiler_params=_params("parallel", "arbitrary"),
        name="norm_proj",
    )(x, g, w)


def _moba_kernel(slope_ref, q_ref, k_ref, v_ref, o_ref, kaug_ref, vaug_ref, kmean_ref, qaug_ref, m_ref, acc_ref,
                 *, n_blocks, sub):
    h = pl.program_id(1)
    qi = pl.program_id(2)
    blk = MOBA_BLOCK
    dh = HEAD_DIM
    tq = sub * blk
    inv_scale = dh ** 0.5
    c2 = LOG2E / inv_scale
    bias_lane = LANES - 3

    @pl.when(qi == 0)
    def _():
        lane = lax.broadcasted_iota(jnp.int32, (blk, LANES), 1)
        pos = lax.broadcasted_iota(jnp.int32, (blk, LANES), 0)
        unit = slope_ref[h] * inv_scale
        kmean_ref[...] = jnp.zeros_like(kmean_ref)
        for n in range(n_blocks):
            rows = slice(n * blk, (n + 1) * blk)
            beta = (pos + n * blk).astype(F32) * unit
            b_hi = beta.astype(BF16).astype(F32)
            b_mid = (beta - b_hi).astype(BF16).astype(F32)
            b_lo = beta - b_hi - b_mid
            base = jnp.where(lane == bias_lane, b_hi,
                             jnp.where(lane == bias_lane + 1, b_mid,
                                       jnp.where(lane == bias_lane + 2, b_lo, 0.0)))
            kb = k_ref[rows, :]
            kaug_ref[rows, 0:dh] = kb
            kaug_ref[rows, dh:2 * dh] = jnp.where(lane == n, 1.0, base).astype(BF16)
            kmean_ref[n:n + 1, :] = jnp.sum(kb.astype(F32), axis=0, keepdims=True) * (1.0 / blk)
            vaug_ref[rows, 0:dh] = v_ref[rows, :]
            vaug_ref[rows, dh:2 * dh] = jnp.ones((blk, dh), BF16)

    q = q_ref[...]
    n_rows = min(LANES, -(-n_blocks // 16) * 16)
    kmean = kmean_ref[0:n_rows, :]
    km_hi = kmean.astype(BF16)
    km_lo = (kmean - km_hi.astype(F32)).astype(BF16)
    gate = _dot_nt(km_hi, q) + _dot_nt(km_lo, q)
    block = lax.broadcasted_iota(jnp.int32, (n_rows, tq), 0)
    own = qi * sub + lax.broadcasted_iota(jnp.int32, (n_rows, tq), 1) // blk
    neg_inf = jnp.float32(-jnp.inf)
    gate = jnp.where(block < own, gate, neg_inf)
    chosen = block == own
    for _ in range(MOBA_TOPK):
        top = jnp.max(gate, axis=0, keepdims=True)
        first = jnp.min(jnp.where(gate == top, block, LANES), axis=0, keepdims=True)
        pick = (block == first) & (block < own)
        chosen = chosen | pick
        gate = jnp.where(pick, neg_inf, gate)
    chosen = chosen.astype(F32)
    if n_rows < LANES:
        chosen = jnp.concatenate([chosen, jnp.zeros((LANES - n_rows, tq), F32)], axis=0)
    chosen_rows = jnp.transpose(chosen)
    lane = lax.broadcasted_iota(jnp.int32, (tq, LANES), 1)
    extra = jnp.where(lane >= bias_lane, 1.0, jnp.where(chosen_rows > 0.5, 0.0, MASKED))
    qaug_ref[:, 0:dh] = q
    qaug_ref[:, dh:2 * dh] = extra.astype(BF16)
    m_ref[...] = jnp.full_like(m_ref, MASKED)
    acc_ref[...] = jnp.zeros_like(acc_ref)

    def scores(row_lo, n, causal):
        start = pl.multiple_of(n * blk, blk)
        s = _dot_nt(qaug_ref[row_lo:tq, :], kaug_ref[pl.ds(start, blk), :])
        if causal:
            r = lax.broadcasted_iota(jnp.int32, s.shape, 0)
            c = lax.broadcasted_iota(jnp.int32, s.shape, 1)
            s = jnp.where(c <= r, s, MASKED)
        return s

    def max_pass(row_lo, n, causal):
        m_ref[row_lo:tq, :] = jnp.maximum(m_ref[row_lo:tq, :], scores(row_lo, n, causal))

    def sum_pass(row_lo, n, causal):
        start = pl.multiple_of(n * blk, blk)
        p = jnp.exp2(scores(row_lo, n, causal) * c2 - m_ref[row_lo:tq, :])
        acc_ref[row_lo:tq, :] += _dot(p.astype(BF16), vaug_ref[pl.ds(start, blk), :])

    def all_blocks(visit):
        for kk in range(sub):
            visit(kk * blk, qi * sub + kk, True)

        def body(i, carry):
            for u in range(sub):
                visit(0, i * sub + u, False)
            return carry

        lax.fori_loop(0, qi, body, 0)

    all_blocks(max_pass)
    m_ref[...] = jnp.broadcast_to(jnp.max(m_ref[...], axis=1, keepdims=True) * c2, m_ref.shape)
    all_blocks(sum_pass)
    acc = acc_ref[...]
    o_ref[...] = (acc[:, 0:dh] / acc[:, dh:2 * dh]).astype(o_ref.dtype)


def _moba(proj, slopes, *, batch, seq, heads, sub, q_col, k_col, v_col):
    n_blocks = seq // MOBA_BLOCK
    assert n_blocks <= LANES - 3 and n_blocks % sub == 0
    tq = sub * MOBA_BLOCK
    kernel = functools.partial(_moba_kernel, n_blocks=n_blocks, sub=sub)
    return pl.pallas_call(
        kernel,
        grid=(batch, heads, n_blocks // sub),
        in_specs=[
            pl.BlockSpec(memory_space=pltpu.SMEM),
            pl.BlockSpec((None, tq, HEAD_DIM), lambda b, h, i: (b, i, q_col + h)),
            pl.BlockSpec((None, seq, HEAD_DIM), lambda b, h, i: (b, 0, k_col + h)),
            pl.BlockSpec((None, seq, HEAD_DIM), lambda b, h, i: (b, 0, v_col + h)),
        ],
        out_specs=pl.BlockSpec((None, tq, HEAD_DIM), lambda b, h, i: (b, i, h)),
        out_shape=jax.ShapeDtypeStruct((batch, seq, heads * HEAD_DIM), BF16),
        scratch_shapes=[
            pltpu.VMEM((seq, 2 * HEAD_DIM), BF16),
            pltpu.VMEM((seq, 2 * HEAD_DIM), BF16),
            pltpu.VMEM((LANES, HEAD_DIM), F32),
            pltpu.VMEM((tq, 2 * HEAD_DIM), BF16),
            pltpu.VMEM((tq, MOBA_BLOCK), F32),
            pltpu.VMEM((tq, 2 * HEAD_DIM), F32),
        ],
        compiler_params=_params("parallel", "parallel", "arbitrary"),
        name="moba",
    )(slopes, proj, proj, proj)


def _sb_kernel(q_ref, k_ref, v_ref, o_ref, run_ref, acc_ref, *, sub):
    qi = pl.program_id(2)
    tile = SB_TILE
    tq = sub * tile
    c2 = LOG2E * HEAD_DIM ** -0.5
    row2 = lax.broadcasted_iota(jnp.int32, (2 * tile, tile), 0)
    col2 = lax.broadcasted_iota(jnp.int32, (2 * tile, tile), 1)
    suffix2 = ((row2 % tile) > col2).astype(BF16)
    sign_bit = jnp.uint32(0x80000000)
    run_ref[...] = jnp.zeros_like(run_ref)
    acc_ref[...] = jnp.zeros_like(acc_ref)

    def sweep(row_lo, j, diagonal):
        rows = slice(row_lo, tq)
        start = pl.multiple_of(j * tile, tile)
        z2 = _dot_nt(q_ref[rows, :], k_ref[pl.ds(start, tile), :]) * c2
        neg_abs = lax.bitcast_convert_type(lax.bitcast_convert_type(z2, jnp.uint32) | sign_bit, F32)
        cost = jnp.maximum(z2, 0.0) + jnp.log(1.0 + jnp.exp2(neg_abs)) * LOG2E
        if diagonal:
            r = lax.broadcasted_iota(jnp.int32, z2.shape, 0)
            c = lax.broadcasted_iota(jnp.int32, z2.shape, 1)
            strict = c < r
            cost = jnp.where(strict, cost, 0.0)
        hi = cost.astype(BF16)
        lo = (cost - hi.astype(F32)).astype(BF16)
        later = _dot(jnp.concatenate([hi, lo], axis=1), suffix2)
        w = jnp.exp2(z2 - cost - later - run_ref[rows, :])
        if diagonal:
            w = jnp.where(strict, w, 0.0)
        acc_ref[rows, :] += _dot(w.astype(BF16), v_ref[pl.ds(start, tile), :])
        run_ref[rows, :] += jnp.sum(cost, axis=1, keepdims=True)

    for kk in reversed(range(sub)):
        sweep(kk * tile, qi * sub + kk, True)

    def alive():
        return jnp.min(run_ref[...]) <= SB_DEAD_BITS

    def cond(carry):
        step, go = carry
        return jnp.logical_and(step < qi * sub, go)

    def body(carry):
        step, _ = carry
        sweep(0, qi * sub - 1 - step, False)
        return step + 1, alive()

    lax.while_loop(cond, body, (jnp.int32(0), alive()))
    o_ref[...] = acc_ref[...].astype(o_ref.dtype)


def _stick_breaking(proj, *, batch, seq, heads, sub, q_col, k_col, v_col):
    n_tiles = seq // SB_TILE
    assert n_tiles % sub == 0
    tq = sub * SB_TILE
    return pl.pallas_call(
        functools.partial(_sb_kernel, sub=sub),
        grid=(batch, heads, n_tiles // sub),
        in_specs=[
            pl.BlockSpec((None, tq, HEAD_DIM), lambda b, h, i: (b, i, q_col + h)),
            pl.BlockSpec((None, seq, HEAD_DIM), lambda b, h, i: (b, 0, k_col + h)),
            pl.BlockSpec((None, seq, HEAD_DIM), lambda b, h, i: (b, 0, v_col + h)),
        ],
        out_specs=pl.BlockSpec((None, tq, HEAD_DIM), lambda b, h, i: (b, i, h)),
        out_shape=jax.ShapeDtypeStruct((batch, seq, heads * HEAD_DIM), BF16),
        scratch_shapes=[pltpu.VMEM((tq, 1), F32), pltpu.VMEM((tq, HEAD_DIM), F32)],
        compiler_params=_params("parallel", "parallel", "arbitrary"),
        name="stick_breaking",
    )(proj, proj, proj)


def _mix_kernel(a_ref, b_ref, wa_ref, wb_ref, ga_ref, gb_ref, o_ref):
    ya = _dot(a_ref[...], wa_ref[...])
    yb = _dot(b_ref[...], wb_ref[...])
    ga = jax.nn.sigmoid(ga_ref[...].astype(F32))
    gb = jax.nn.sigmoid(gb_ref[...].astype(F32))
    o_ref[...] = (ga * ya + gb * yb).astype(o_ref.dtype)


def _mix(att_a, att_b, wa, wb, proj, *, ga_col, gb_col, tm, tn):
    n, ka = att_a.shape
    kb = att_b.shape[1]
    d = wa.shape[1]
    return pl.pallas_call(
        _mix_kernel,
        grid=(n // tm, d // tn),
        in_specs=[
            pl.BlockSpec((tm, ka), lambda i, j: (i, 0)),
            pl.BlockSpec((tm, kb), lambda i, j: (i, 0)),
            pl.BlockSpec((ka, tn), lambda i, j: (0, j)),
            pl.BlockSpec((kb, tn), lambda i, j: (0, j)),
            pl.BlockSpec((tm, tn), lambda i, j: (i, ga_col // tn + j)),
            pl.BlockSpec((tm, tn), lambda i, j: (i, gb_col // tn + j)),
        ],
        out_specs=pl.BlockSpec((tm, tn), lambda i, j: (i, j)),
        out_shape=jax.ShapeDtypeStruct((n, d), BF16),
        compiler_params=_params("parallel", "arbitrary"),
        name="mix",
    )(att_a, att_b, wa, wb, proj, proj)


def _matmul_residual_kernel(a_ref, w_ref, r_ref, o_ref):
    o_ref[...] = r_ref[...] + _dot(a_ref[...], w_ref[...])


def _matmul_residual(a, w, r, *, tm, tn):
    n, k = a.shape
    d = w.shape[1]
    return pl.pallas_call(
        _matmul_residual_kernel,
        grid=(n // tm, d // tn),
        in_specs=[
            pl.BlockSpec((tm, k), lambda i, j: (i, 0)),
            pl.BlockSpec((k, tn), lambda i, j: (0, j)),
            pl.BlockSpec((tm, tn), lambda i, j: (i, j)),
        ],
        out_specs=pl.BlockSpec((tm, tn), lambda i, j: (i, j)),
        out_shape=jax.ShapeDtypeStruct((n, d), F32),
        compiler_params=_params("parallel", "arbitrary"),
        name="matmul_residual",
    )(a, w, r)


def _ple_kernel(h_ref, hc_ref, p_ref, g_ref, wg_ref, wp_ref, fg_ref, o_ref, xn_ref, ssq_ref):
    j = pl.program_id(1)
    nj = pl.num_programs(1)
    tn = hc_ref.shape[1]

    @pl.when(j == 0)
    def _():
        _rmsnorm_rows(h_ref, g_ref, xn_ref)
        ssq_ref[...] = jnp.zeros_like(ssq_ref)

    gate = jax.nn.sigmoid(_dot(xn_ref[...], wg_ref[...]))
    emb = _dot(p_ref[...].astype(BF16), wp_ref[...])
    new = hc_ref[...] + gate * emb
    o_ref[:, pl.ds(pl.multiple_of(j * tn, tn), tn)] = new
    ssq_ref[...] += jnp.sum(new * new, axis=-1, keepdims=True)

    @pl.when(j == nj - 1)
    def _():
        rows = o_ref.shape[0]
        chunk = min(NORM_ROWS, rows)

        def body(r, carry):
            sl = pl.ds(pl.multiple_of(r * chunk, chunk), chunk)
            inv = lax.rsqrt(ssq_ref[sl, :] * (1.0 / o_ref.shape[1]) + RMS_EPS)
            o_ref[sl, :] = (o_ref[sl, :] * inv) * fg_ref[...]
            return carry

        lax.fori_loop(0, rows // chunk, body, 0)


def _ple(h, p, g, wg, wp, fg, *, tm, tn):
    n, d = h.shape
    pd = p.shape[1]
    return pl.pallas_call(
        _ple_kernel,
        grid=(n // tm, d // tn),
        in_specs=[
            _row_block(tm, d, buffers=2),
            pl.BlockSpec((tm, tn), lambda i, j: (i, j)),
            pl.BlockSpec((tm, pd), lambda i, j: (i, 0)),
            pl.BlockSpec((1, d), lambda i, j: (0, 0)),
            pl.BlockSpec((d, tn), lambda i, j: (0, j)),
            pl.BlockSpec((pd, tn), lambda i, j: (0, j)),
            pl.BlockSpec((1, d), lambda i, j: (0, 0)),
        ],
        out_specs=_row_block(tm, d, buffers=2),
        out_shape=jax.ShapeDtypeStruct((n, d), F32),
        scratch_shapes=[
            pltpu.VMEM((tm, d), BF16),
            pltpu.VMEM((tm, 1), F32),
        ],
        compiler_params=_params("parallel", "arbitrary"),
        name="ple_final_norm",
    )(h, h, p, g, wg, wp, fg)


def _tile(n, want):
    t = min(n, want)
    assert n % t == 0, (n, want)
    return t


def kernel(x, p, ffn1_norm, ffn1_w_gate, ffn1_w_up, ffn1_w_down, mix_norm, w_in, w_branch_moba, w_branch_sb, w_out, ffn2_norm, ffn2_w_gate, ffn2_w_up, ffn2_w_down, ple_norm, w_ple_gate, w_ple_proj, final_norm):
    batch, seq, d = x.shape
    depth = p.shape[0]
    n = batch * seq
    moba_w = w_branch_moba.shape[1]
    sb_w = w_branch_sb.shape[1]
    assert moba_w % HEAD_DIM == 0 and sb_w % HEAD_DIM == 0
    assert seq % MOBA_BLOCK == 0 and seq % SB_TILE == 0
    assert w_in.shape[2] == 3 * moba_w + 3 * sb_w + 2 * d
    moba_heads = moba_w // HEAD_DIM
    sb_heads = sb_w // HEAD_DIM
    slopes = jnp.asarray(2.0 ** (-8.0 * np.arange(1, moba_heads + 1) / moba_heads), dtype=F32)
    off = np.cumsum([0, moba_w, moba_w, moba_w, sb_w, sb_w, sb_w, d])
    col = [int(o) // HEAD_DIM for o in off]

    tm = _tile(n, 512)
    tm_wide = _tile(n, 1024)
    tn = _tile(d, 1024)
    tf = _tile(ffn1_w_gate.shape[2], 256)
    row = lambda v: v.reshape(1, -1)
    bf = lambda w: w.astype(BF16)

    sub = min(SUPER_TILES, seq // MOBA_BLOCK)
    assert depth == 1
    i = 0
    h = x.reshape(n, d)
    h = _ffn(h, row(ffn1_norm[i]), bf(ffn1_w_gate[i]), bf(ffn1_w_up[i]), bf(ffn1_w_down[i]), tm=tm_wide, tf=tf)
    proj = _norm_proj(h, row(mix_norm[i]), bf(w_in[i]), tm=tm_wide, tn=tn)
    proj3 = proj.reshape(batch, seq, proj.shape[1])
    att_a = _moba(proj3, slopes, batch=batch, seq=seq, heads=moba_heads, sub=sub,
                  q_col=col[0], k_col=col[1], v_col=col[2])
    att_b = _stick_breaking(proj3, batch=batch, seq=seq, heads=sb_heads, sub=min(SB_SUPER_TILES, seq // SB_TILE),
                            q_col=col[3], k_col=col[4], v_col=col[5])
    mixed = _mix(att_a.reshape(n, moba_w), att_b.reshape(n, sb_w), bf(w_branch_moba[i]), bf(w_branch_sb[i]),
                 proj, ga_col=int(off[6]), gb_col=int(off[7]), tm=tm_wide, tn=tn)
    h = _matmul_residual(mixed, bf(w_out[i]), h, tm=tm_wide, tn=tn)
    h = _ffn(h, row(ffn2_norm[i]), bf(ffn2_w_gate[i]), bf(ffn2_w_up[i]), bf(ffn2_w_down[i]), tm=tm_wide, tf=tf)
    h = _ple(h, p[i].reshape(n, -1), row(ple_norm[i]), bf(w_ple_gate[i]), bf(w_ple_proj[i]), row(final_norm),
             tm=tm, tn=_tile(d, 512))
    return h.reshape(batch, seq, d)
```

```python
import functools

import numpy as np
import jax
import jax.numpy as jnp
from jax import lax
from jax.experimental import pallas as pl
from jax.experimental.pallas import tpu as pltpu

F32 = jnp.float32
BF16 = jnp.bfloat16

HEAD_DIM = 128
MOBA_BLOCK = 256
MOBA_TOPK = 3
SB_TILE = 256
RMS_EPS = 1e-6
LANES = 128
NORM_ROWS = 64
VMEM_LIMIT_BYTES = 60 * 1024 * 1024
SUPER_TILES = 4
SB_SUPER_TILES = 4
LOG2E = 1.4426950408889634
MASKED = -(2.0 ** 100)
SB_DEAD_BITS = 150.0


def _dot(a, b):
    return jnp.dot(a, b, preferred_element_type=F32)


def _dot_nt(a, b):
    return lax.dot_general(a, b, (((1,), (1,)), ((), ())), preferred_element_type=F32)


def _params(*semantics):
    return pltpu.CompilerParams(dimension_semantics=semantics, vmem_limit_bytes=VMEM_LIMIT_BYTES)


def _row_block(tm, d, buffers=1):
    return pl.BlockSpec((tm, d), lambda i, j: (i, 0), pipeline_mode=pl.Buffered(buffers))


def _rmsnorm_rows(x_ref, g_ref, out_ref):
    rows = x_ref.shape[0]
    chunk = min(NORM_ROWS, rows)

    def body(r, carry):
        sl = pl.ds(pl.multiple_of(r * chunk, chunk), chunk)
        x = x_ref[sl, :]
        ms = jnp.mean(x * x, axis=-1, keepdims=True)
        out_ref[sl, :] = ((x * lax.rsqrt(ms + RMS_EPS)) * g_ref[...]).astype(out_ref.dtype)
        return carry

    lax.fori_loop(0, rows // chunk, body, 0)


def _ffn_kernel(x_ref, g_ref, wg_ref, wu_ref, wd_ref, o_ref, xn_ref):
    j = pl.program_id(1)

    @pl.when(j == 0)
    def _():
        _rmsnorm_rows(x_ref, g_ref, xn_ref)
        o_ref[...] = jnp.zeros_like(o_ref)

    xn = xn_ref[...]
    gate = _dot(xn, wg_ref[...])
    up = _dot(xn, wu_ref[...])
    act = (gate * jax.nn.sigmoid(gate) * up).astype(BF16)
    o_ref[...] += _dot(act, wd_ref[...])

    @pl.when(j == pl.num_programs(1) - 1)
    def _():
        o_ref[...] = x_ref[...] + 0.5 * o_ref[...]


def _ffn(x, g, wg, wu, wd, *, tm, tf, buffers=1):
    n, d = x.shape
    dff = wg.shape[1]
    return pl.pallas_call(
        _ffn_kernel,
        grid=(n // tm, dff // tf),
        in_specs=[
            _row_block(tm, d, buffers),
            pl.BlockSpec((1, d), lambda i, j: (0, 0)),
            pl.BlockSpec((d, tf), lambda i, j: (0, j)),
            pl.BlockSpec((d, tf), lambda i, j: (0, j)),
            pl.BlockSpec((tf, d), lambda i, j: (j, 0)),
        ],
        out_specs=_row_block(tm, d, buffers),
        out_shape=jax.ShapeDtypeStruct((n, d), F32),
        scratch_shapes=[pltpu.VMEM((tm, d), BF16)],
        compiler_params=_params("parallel", "arbitrary"),
        name="ffn",
    )(x, g, wg, wu, wd)


def _norm_proj_kernel(x_ref, g_ref, w_ref, o_ref, xn_ref):
    @pl.when(pl.program_id(1) == 0)
    def _():
        _rmsnorm_rows(x_ref, g_ref, xn_ref)

    o_ref[...] = _dot(xn_ref[...], w_ref[...]).astype(o_ref.dtype)


def _norm_proj(x, g, w, *, tm, tn, buffers=1):
    n, d = x.shape
    c = w.shape[1]
    return pl.pallas_call(
        _norm_proj_kernel,
        grid=(n // tm, c // tn),
        in_specs=[
            _row_block(tm, d, buffers),
            pl.BlockSpec((1, d), lambda i, j: (0, 0)),
            pl.BlockSpec((d, tn), lambda i, j: (0, j)),
        ],
        out_specs=pl.BlockSpec((tm, tn), lambda i, j: (i, j)),
        out_shape=jax.ShapeDtypeStruct((n, c), BF16),
        scratch_shapes=[pltpu.VMEM((tm, d), BF16)],
        compiler_params=_params("parallel", "arbitrary"),
        name="norm_proj",
    )(x, g, w)


def _moba_kernel(slope_ref, q_ref, k_ref, v_ref, o_ref, kaug_ref, vaug_ref, kmean_ref, qaug_ref, m_ref, acc_ref,
                 *, n_blocks, sub):
    h = pl.program_id(1)
    qi = pl.program_id(2)
    blk = MOBA_BLOCK
    dh = HEAD_DIM
    tq = sub * blk
    inv_scale = dh ** 0.5
    c2 = LOG2E / inv_scale
    bias_lane = LANES - 3

    @pl.when(qi == 0)
    def _():
        lane = lax.broadcasted_iota(jnp.int32, (blk, LANES), 1)
        pos = lax.broadcasted_iota(jnp.int32, (blk, LANES), 0)
        unit = slope_ref[h] * inv_scale
        kmean_ref[...] = jnp.zeros_like(kmean_ref)
        for n in range(n_blocks):
            rows = slice(n * blk, (n + 1) * blk)
            beta = (pos + n * blk).astype(F32) * unit
            b_hi = beta.astype(BF16).astype(F32)
            b_mid = (beta - b_hi).astype(BF16).astype(F32)
            b_lo = beta - b_hi - b_mid
            base = jnp.where(lane == bias_lane, b_hi,
                             jnp.where(lane == bias_lane + 1, b_mid,
                                       jnp.where(lane == bias_lane + 2, b_lo, 0.0)))
            kb = k_ref[rows, :]
            kaug_ref[rows, 0:dh] = kb
            kaug_ref[rows, dh:2 * dh] = jnp.where(lane == n, 1.0, base).astype(BF16)
            kmean_ref[n:n + 1, :] = jnp.sum(kb.astype(F32), axis=0, keepdims=True) * (1.0 / blk)
            vaug_ref[rows, 0:dh] = v_ref[rows, :]
            vaug_ref[rows, dh:2 * dh] = jnp.ones((blk, dh), BF16)

    q = q_ref[...]
    n_rows = min(LANES, -(-n_blocks // 16) * 16)
    kmean = kmean_ref[0:n_rows, :]
    km_hi = kmean.astype(BF16)
    km_lo = (kmean - km_hi.astype(F32)).astype(BF16)
    gate = _dot_nt(km_hi, q) + _dot_nt(km_lo, q)
    block = lax.broadcasted_iota(jnp.int32, (n_rows, tq), 0)
    own = qi * sub + lax.broadcasted_iota(jnp.int32, (n_rows, tq), 1) // blk
    neg_inf = jnp.float32(-jnp.inf)
    gate = jnp.where(block < own, gate, neg_inf)
    chosen = block == own
    for _ in range(MOBA_TOPK):
        top = jnp.max(gate, axis=0, keepdims=True)
        first = jnp.min(jnp.where(gate == top, block, LANES), axis=0, keepdims=True)
        pick = (block == first) & (block < own)
        chosen = chosen | pick
        gate = jnp.where(pick, neg_inf, gate)
    chosen = chosen.astype(F32)
    if n_rows < LANES:
        chosen = jnp.concatenate([chosen, jnp.zeros((LANES - n_rows, tq), F32)], axis=0)
    chosen_rows = jnp.transpose(chosen)
    lane = lax.broadcasted_iota(jnp.int32, (tq, LANES), 1)
    extra = jnp.where(lane >= bias_lane, 1.0, jnp.where(chosen_rows > 0.5, 0.0, MASKED))
    qaug_ref[:, 0:dh] = q
    qaug_ref[:, dh:2 * dh] = extra.astype(BF16)
    m_ref[...] = jnp.full_like(m_ref, MASKED)
    acc_ref[...] = jnp.zeros_like(acc_ref)

    def scores(row_lo, n, causal):
        start = pl.multiple_of(n * blk, blk)
        s = _dot_nt(qaug_ref[row_lo:tq, :], kaug_ref[pl.ds(start, blk), :])
        if causal:
            r = lax.broadcasted_iota(jnp.int32, s.shape, 0)
            c = lax.broadcasted_iota(jnp.int32, s.shape, 1)
            s = jnp.where(c <= r, s, MASKED)
        return s

    def max_pass(row_lo, n, causal):
        m_ref[row_lo:tq, :] = jnp.maximum(m_ref[row_lo:tq, :], scores(row_lo, n, causal))

    def sum_pass(row_lo, n, causal):
        start = pl.multiple_of(n * blk, blk)
        p = jnp.exp2(scores(row_lo, n, causal) * c2 - m_ref[row_lo:tq, :])
        acc_ref[row_lo:tq, :] += _dot(p.astype(BF16), vaug_ref[pl.ds(start, blk), :])

    def all_blocks(visit):
        for kk in range(sub):
            visit(kk * blk, qi * sub + kk, True)

        def body(i, carry):
            for u in range(sub):
                visit(0, i * sub + u, False)
            return carry

        lax.fori_loop(0, qi, body, 0)

    all_blocks(max_pass)
    m_ref[...] = jnp.broadcast_to(jnp.max(m_ref[...], axis=1, keepdims=True) * c2, m_ref.shape)
    all_blocks(sum_pass)
    acc = acc_ref[...]
    o_ref[...] = (acc[:, 0:dh] / acc[:, dh:2 * dh]).astype(o_ref.dtype)


def _moba(proj, slopes, *, batch, seq, heads, sub, q_col, k_col, v_col):
    n_blocks = seq // MOBA_BLOCK
    assert n_blocks <= LANES - 3 and n_blocks % sub == 0
    tq = sub * MOBA_BLOCK
    kernel = functools.partial(_moba_kernel, n_blocks=n_blocks, sub=sub)
    return pl.pallas_call(
        kernel,
        grid=(batch, heads, n_blocks // sub),
        in_specs=[
            pl.BlockSpec(memory_space=pltpu.SMEM),
            pl.BlockSpec((None, tq, HEAD_DIM), lambda b, h, i: (b, i, q_col + h)),
            pl.BlockSpec((None, seq, HEAD_DIM), lambda b, h, i: (b, 0, k_col + h)),
            pl.BlockSpec((None, seq, HEAD_DIM), lambda b, h, i: (b, 0, v_col + h)),
        ],
        out_specs=pl.BlockSpec((None, tq, HEAD_DIM), lambda b, h, i: (b, i, h)),
        out_shape=jax.ShapeDtypeStruct((batch, seq, heads * HEAD_DIM), BF16),
        scratch_shapes=[
            pltpu.VMEM((seq, 2 * HEAD_DIM), BF16),
            pltpu.VMEM((seq, 2 * HEAD_DIM), BF16),
            pltpu.VMEM((LANES, HEAD_DIM), F32),
            pltpu.VMEM((tq, 2 * HEAD_DIM), BF16),
            pltpu.VMEM((tq, MOBA_BLOCK), F32),
            pltpu.VMEM((tq, 2 * HEAD_DIM), F32),
        ],
        compiler_params=_params("parallel", "parallel", "arbitrary"),
        name="moba",
    )(slopes, proj, proj, proj)


def _sb_kernel(q_ref, k_ref, v_ref, o_ref, run_ref, acc_ref, *, sub):
    qi = pl.program_id(2)
    tile = SB_TILE
    tq = sub * tile
    c2 = LOG2E * HEAD_DIM ** -0.5
    row2 = lax.broadcasted_iota(jnp.int32, (2 * tile, tile), 0)
    col2 = lax.broadcasted_iota(jnp.int32, (2 * tile, tile), 1)
    suffix2 = ((row2 % tile) > col2).astype(BF16)
    sign_bit = jnp.uint32(0x80000000)
    run_ref[...] = jnp.zeros_like(run_ref)
    acc_ref[...] = jnp.zeros_like(acc_ref)

    def sweep(row_lo, j, diagonal):
        rows = slice(row_lo, tq)
        start = pl.multiple_of(j * tile, tile)
        z2 = _dot_nt(q_ref[rows, :], k_ref[pl.ds(start, tile), :]) * c2
        neg_abs = lax.bitcast_convert_type(lax.bitcast_convert_type(z2, jnp.uint32) | sign_bit, F32)
        cost = jnp.maximum(z2, 0.0) + jnp.log(1.0 + jnp.exp2(neg_abs)) * LOG2E
        if diagonal:
            r = lax.broadcasted_iota(jnp.int32, z2.shape, 0)
            c = lax.broadcasted_iota(jnp.int32, z2.shape, 1)
            strict = c < r
            cost = jnp.where(strict, cost, 0.0)
        hi = cost.astype(BF16)
        lo = (cost - hi.astype(F32)).astype(BF16)
        later = _dot(jnp.concatenate([hi, lo], axis=1), suffix2)
        w = jnp.exp2(z2 - cost - later - run_ref[rows, :])
        if diagonal:
            w = jnp.where(strict, w, 0.0)
        acc_ref[rows, :] += _dot(w.astype(BF16), v_ref[pl.ds(start, tile), :])
        run_ref[rows, :] += jnp.sum(cost, axis=1, keepdims=True)

    for kk in reversed(range(sub)):
        sweep(kk * tile, qi * sub + kk, True)

    def alive():
        return jnp.min(run_ref[...]) <= SB_DEAD_BITS

    def cond(carry):
        step, go = carry
        return jnp.logical_and(step < qi * sub, go)

    def body(carry):
        step, _ = carry
        sweep(0, qi * sub - 1 - step, False)
        return step + 1, alive()

    lax.while_loop(cond, body, (jnp.int32(0), alive()))
    o_ref[...] = acc_ref[...].astype(o_ref.dtype)


def _stick_breaking(proj, *, batch, seq, heads, sub, q_col, k_col, v_col):
    n_tiles = seq // SB_TILE
    assert n_tiles % sub == 0
    tq = sub * SB_TILE
    return pl.pallas_call(
        functools.partial(_sb_kernel, sub=sub),
        grid=(batch, heads, n_tiles // sub),
        in_specs=[
            pl.BlockSpec((None, tq, HEAD_DIM), lambda b, h, i: (b, i, q_col + h)),
            pl.BlockSpec((None, seq, HEAD_DIM), lambda b, h, i: (b, 0, k_col + h)),
            pl.BlockSpec((None, seq, HEAD_DIM), lambda b, h, i: (b, 0, v_col + h)),
        ],
        out_specs=pl.BlockSpec((None, tq, HEAD_DIM), lambda b, h, i: (b, i, h)),
        out_shape=jax.ShapeDtypeStruct((batch, seq, heads * HEAD_DIM), BF16),
        scratch_shapes=[pltpu.VMEM((tq, 1), F32), pltpu.VMEM((tq, HEAD_DIM), F32)],
        compiler_params=_params("parallel", "parallel", "arbitrary"),
        name="stick_breaking",
    )(proj, proj, proj)


def _mix_kernel(a_ref, b_ref, wa_ref, wb_ref, ga_ref, gb_ref, o_ref):
    ya = _dot(a_ref[...], wa_ref[...])
    yb = _dot(b_ref[...], wb_ref[...])
    ga = jax.nn.sigmoid(ga_ref[...].astype(F32))
    gb = jax.nn.sigmoid(gb_ref[...].astype(F32))
    o_ref[...] = (ga * ya + gb * yb).astype(o_ref.dtype)


def _mix(att_a, att_b, wa, wb, proj, *, ga_col, gb_col, tm, tn):
    n, ka = att_a.shape
    kb = att_b.shape[1]
    d = wa.shape[1]
    return pl.pallas_call(
        _mix_kernel,
        grid=(n // tm, d // tn),
        in_specs=[
            pl.BlockSpec((tm, ka), lambda i, j: (i, 0)),
            pl.BlockSpec((tm, kb), lambda i, j: (i, 0)),
            pl.BlockSpec((ka, tn), lambda i, j: (0, j)),
            pl.BlockSpec((kb, tn), lambda i, j: (0, j)),
            pl.BlockSpec((tm, tn), lambda i, j: (i, ga_col // tn + j)),
            pl.BlockSpec((tm, tn), lambda i, j: (i, gb_col // tn + j)),
        ],
        out_specs=pl.BlockSpec((tm, tn), lambda i, j: (i, j)),
        out_shape=jax.ShapeDtypeStruct((n, d), BF16),
        compiler_params=_params("parallel", "arbitrary"),
        name="mix",
    )(att_a, att_b, wa, wb, proj, proj)


def _matmul_residual_kernel(a_ref, w_ref, r_ref, o_ref):
    o_ref[...] = r_ref[...] + _dot(a_ref[...], w_ref[...])


def _matmul_residual(a, w, r, *, tm, tn):
    n, k = a.shape
    d = w.shape[1]
    return pl.pallas_call(
        _matmul_residual_kernel,
        grid=(n // tm, d // tn),
        in_specs=[
            pl.BlockSpec((tm, k), lambda i, j: (i, 0)),
            pl.BlockSpec((k, tn), lambda i, j: (0, j)),
            pl.BlockSpec((tm, tn), lambda i, j: (i, j)),
        ],
        out_specs=pl.BlockSpec((tm, tn), lambda i, j: (i, j)),
        out_shape=jax.ShapeDtypeStruct((n, d), F32),
        compiler_params=_params("parallel", "arbitrary"),
        name="matmul_residual",
    )(a, w, r)


def _ple_kernel(h_ref, p_ref, g_ref, wg_ref, wp_ref, fg_ref, o_ref, xn_ref, ssq_ref):
    j = pl.program_id(1)
    nj = pl.num_programs(1)
    tn = wg_ref.shape[1]

    @pl.when(j == 0)
    def _():
        _rmsnorm_rows(h_ref, g_ref, xn_ref)
        ssq_ref[...] = jnp.zeros_like(ssq_ref)

    cols = pl.ds(pl.multiple_of(j * tn, tn), tn)
    gate = jax.nn.sigmoid(_dot(xn_ref[...], wg_ref[...]))
    emb = _dot(p_ref[...].astype(BF16), wp_ref[...])
    new = h_ref[:, cols] + gate * emb
    o_ref[:, cols] = new
    ssq_ref[...] += jnp.sum(new * new, axis=-1, keepdims=True)

    @pl.when(j == nj - 1)
    def _():
        rows = o_ref.shape[0]
        chunk = min(NORM_ROWS, rows)

        def body(r, carry):
            sl = pl.ds(pl.multiple_of(r * chunk, chunk), chunk)
            inv = lax.rsqrt(ssq_ref[sl, :] * (1.0 / o_ref.shape[1]) + RMS_EPS)
            o_ref[sl, :] = (o_ref[sl, :] * inv) * fg_ref[...]
            return carry

        lax.fori_loop(0, rows // chunk, body, 0)


def _ple(h, p, g, wg, wp, fg, *, tm, tn):
    n, d = h.shape
    pd = p.shape[1]
    return pl.pallas_call(
        _ple_kernel,
        grid=(n // tm, d // tn),
        in_specs=[
            _row_block(tm, d, buffers=2),
            pl.BlockSpec((tm, pd), lambda i, j: (i, 0)),
            pl.BlockSpec((1, d), lambda i, j: (0, 0)),
            pl.BlockSpec((d, tn), lambda i, j: (0, j)),
            pl.BlockSpec((pd, tn), lambda i, j: (0, j)),
            pl.BlockSpec((1, d), lambda i, j: (0, 0)),
        ],
        out_specs=_row_block(tm, d, buffers=2),
        out_shape=jax.ShapeDtypeStruct((n, d), F32),
        scratch_shapes=[
            pltpu.VMEM((tm, d), BF16),
            pltpu.VMEM((tm, 1), F32),
        ],
        compiler_params=_params("parallel", "arbitrary"),
        name="ple_final_norm",
    )(h, p, g, wg, wp, fg)


def _tile(n, want):
    t = min(n, want)
    assert n % t == 0, (n, want)
    return t


def kernel(x, p, ffn1_norm, ffn1_w_gate, ffn1_w_up, ffn1_w_down, mix_norm, w_in, w_branch_moba, w_branch_sb, w_out, ffn2_norm, ffn2_w_gate, ffn2_w_up, ffn2_w_down, ple_norm, w_ple_gate, w_ple_proj, final_norm):
    batch, seq, d = x.shape
    depth = p.shape[0]
    n = batch * seq
    moba_w = w_branch_moba.shape[1]
    sb_w = w_branch_sb.shape[1]
    assert moba_w % HEAD_DIM == 0 and sb_w % HEAD_DIM == 0
    assert seq % MOBA_BLOCK == 0 and seq % SB_TILE == 0
    assert w_in.shape[2] == 3 * moba_w + 3 * sb_w + 2 * d
    moba_heads = moba_w // HEAD_DIM
    sb_heads = sb_w // HEAD_DIM
    slopes = jnp.asarray(2.0 ** (-8.0 * np.arange(1, moba_heads + 1) / moba_heads), dtype=F32)
    off = np.cumsum([0, moba_w, moba_w, moba_w, sb_w, sb_w, sb_w, d])
    col = [int(o) // HEAD_DIM for o in off]

    tm = _tile(n, 512)
    tm_wide = _tile(n, 1024)
    tn = _tile(d, 1024)
    tf = _tile(ffn1_w_gate.shape[2], 256)
    row = lambda v: v.reshape(1, -1)
    bf = lambda w: w.astype(BF16)

    sub = min(SUPER_TILES, seq // MOBA_BLOCK)
    assert depth == 1
    i = 0
    h = x.reshape(n, d)
    h = _ffn(h, row(ffn1_norm[i]), bf(ffn1_w_gate[i]), bf(ffn1_w_up[i]), bf(ffn1_w_down[i]), tm=tm_wide, tf=tf)
    proj = _norm_proj(h, row(mix_norm[i]), bf(w_in[i]), tm=tm_wide, tn=tn)
    proj3 = proj.reshape(batch, seq, proj.shape[1])
    att_a = _moba(proj3, slopes, batch=batch, seq=seq, heads=moba_heads, sub=sub,
                  q_col=col[0], k_col=col[1], v_col=col[2])
    att_b = _stick_breaking(proj3, batch=batch, seq=seq, heads=sb_heads, sub=min(SB_SUPER_TILES, seq // SB_TILE),
                            q_col=col[3], k_col=col[4], v_col=col[5])
    mixed = _mix(att_a.reshape(n, moba_w), att_b.reshape(n, sb_w), bf(w_branch_moba[i]), bf(w_branch_sb[i]),
                 proj, ga_col=int(off[6]), gb_col=int(off[7]), tm=tm_wide, tn=tn)
    h = _matmul_residual(mixed, bf(w_out[i]), h, tm=tm_wide, tn=tn)
    h = _ffn(h, row(ffn2_norm[i]), bf(ffn2_w_gate[i]), bf(ffn2_w_up[i]), bf(ffn2_w_down[i]), tm=tm_wide, tf=tf)
    h = _ple(h, p[i].reshape(n, -1), row(ple_norm[i]), bf(w_ple_gate[i]), bf(w_ple_proj[i]), row(final_norm),
             tm=tm, tn=tn)
    return h.reshape(batch, seq, d)
```
